```python
import math
import jax
import jax.numpy as jnp
from jax import lax
import numpy as np

D_MODEL = 1024
BATCH = 4
SEQ = 8192
DEPTH = 1

SSM_WIDTH = D_MODEL // 2
SSM_GROUP = 16
SSM_GROUPS = SSM_WIDTH // SSM_GROUP
SSM_STATE = 64
DT_MIN = 1e-3
DT_MAX = 1e-1
ATTN_HEADS = 8
HEAD_DIM = 64
ATTN_WIDTH = ATTN_HEADS * HEAD_DIM
MOBA_BLOCK = 256
MOBA_TOPK = 3
Q_CHUNK = 64
ROPE_THETA = 500000.0
ROT_DIM = HEAD_DIM // 4
IN_WIDTH = SSM_WIDTH + 3 * ATTN_WIDTH + 2 * D_MODEL
N_EXPERTS = 32
TOP_K = 4
D_FF = D_MODEL
SWIGLU_LIMIT = 7.0
SWIGLU_ALPHA = 1.702
EXPERT_ROWS = 256
RMS_EPS = 1e-5

kernel_name = 'hybrid_s5_moba_moe_block'


def rms_norm(x, g):
    xf = x.astype(jnp.float32)
    r = lax.rsqrt(jnp.mean(xf * xf, axis=-1, keepdims=True) + RMS_EPS)
    return (xf * r * g.astype(jnp.float32)).astype(x.dtype)


def modulate(h, shift, scale):
    return h * (1.0 + scale[:, None, :]) + shift[:, None, :]


def rotary_tables(seq):
    inv_freq = 1.0 / (ROPE_THETA ** (jnp.arange(0, ROT_DIM, 2, dtype=jnp.float32) / ROT_DIM))
    ang = jnp.arange(seq, dtype=jnp.float32)[:, None] * inv_freq[None, :]
    return jnp.cos(ang), jnp.sin(ang)


def partial_rotary(t, cos, sin):
    half = ROT_DIM // 2
    t1 = t[..., :half].astype(jnp.float32)
    t2 = t[..., half:ROT_DIM].astype(jnp.float32)
    r1 = (t1 * cos - t2 * sin).astype(t.dtype)
    r2 = (t2 * cos + t1 * sin).astype(t.dtype)
    return jnp.concatenate([r1, r2, t[..., ROT_DIM:]], axis=-1)


def to_heads(t):
    b, s, _ = t.shape
    return t.reshape(b, s, ATTN_HEADS, HEAD_DIM).transpose(0, 2, 1, 3)


def s5_mixer(u, a_re, a_im, log_dt, b_re, b_im, c_re, c_im, d_skip):
    bsz, s, _ = u.shape
    f32 = jnp.float32
    uf = u.astype(f32).reshape(bsz, s, SSM_GROUPS, SSM_GROUP)
    dt = jnp.exp(log_dt.astype(f32))[:, None]
    lr = a_re.astype(f32)
    li = a_im.astype(f32)
    mag = jnp.exp(lr * dt)
    abar_r = mag * jnp.cos(li * dt)
    abar_i = mag * jnp.sin(li * dt)
    den = lr * lr + li * li
    coef_r = ((abar_r - 1.0) * lr + abar_i * li) / den
    coef_i = (abar_i * lr - (abar_r - 1.0) * li) / den
    br = b_re.astype(f32)
    bi = b_im.astype(f32)
    bbar_r = coef_r[..., None] * br - coef_i[..., None] * bi
    bbar_i = coef_r[..., None] * bi + coef_i[..., None] * br
    bu_r = jnp.einsum('bsgc,gpc->bsgp', uf, bbar_r)
    bu_i = jnp.einsum('bsgc,gpc->bsgp', uf, bbar_i)
    a_r = jnp.broadcast_to(abar_r[None, None], (1, s, SSM_GROUPS, SSM_STATE))
    a_i = jnp.broadcast_to(abar_i[None, None], (1, s, SSM_GROUPS, SSM_STATE))

    def combine(e1, e2):
        a1r, a1i, b1r, b1i = e1
        a2r, a2i, b2r, b2i = e2
        return (a2r * a1r - a2i * a1i,
                a2r * a1i + a2i * a1r,
                a2r * b1r - a2i * b1i + b2r,
                a2r * b1i + a2i * b1r + b2i)

    _, _, xr, xi = lax.associative_scan(combine, (a_r, a_i, bu_r, bu_i), axis=1)
    y = (jnp.einsum('gcp,bsgp->bsgc', c_re.astype(f32), xr)
         - jnp.einsum('gcp,bsgp->bsgc', c_im.astype(f32), xi))
    y = y.reshape(bsz, s, SSM_WIDTH) + d_skip.astype(f32) * uf.reshape(bsz, s, SSM_WIDTH)
    return y.astype(u.dtype)


def moba_attention(q, k, v):
    bsz, nh, s, dh = q.shape
    f32 = jnp.float32
    nb = -(-s // MOBA_BLOCK)
    pad = nb * MOBA_BLOCK - s
    kb = jnp.pad(k, ((0, 0), (0, 0), (0, pad), (0, 0))).reshape(bsz, nh, nb, MOBA_BLOCK, dh)
    vb = jnp.pad(v, ((0, 0), (0, 0), (0, pad), (0, 0))).reshape(bsz, nh, nb, MOBA_BLOCK, dh)
    k_mean = jnp.mean(kb.astype(f32), axis=3)
    pos = jnp.arange(s)
    n_past = pos // MOBA_BLOCK
    gate = jnp.einsum('bhsd,bhnd->bhsn', q.astype(f32), k_mean)
    past = jnp.arange(nb)[None, :] < n_past[:, None]
    gate = jnp.where(past, gate, -jnp.inf)
    n_sel = min(MOBA_TOPK, nb)
    _, sel = lax.top_k(gate, n_sel)
    sel_ok = sel < n_past[:, None]
    n_chunks = s // Q_CHUNK

    def to_chunks(t):
        return jnp.moveaxis(t.reshape(bsz, nh, n_chunks, Q_CHUNK, t.shape[-1]), 2, 0)

    b_ix = jnp.arange(bsz)[:, None, None, None]
    h_ix = jnp.arange(nh)[None, :, None, None]
    scale = dh ** -0.5
    n_sel_keys = n_sel * MOBA_BLOCK

    def chunk(args):
        ci, qc, selc, okc = args
        q0 = ci * Q_CHUNK
        own = q0 // MOBA_BLOCK
        k_own = lax.dynamic_index_in_dim(kb, own, axis=2, keepdims=False)
        v_own = lax.dynamic_index_in_dim(vb, own, axis=2, keepdims=False)
        k_sel = kb[b_ix, h_ix, selc]
        v_sel = vb[b_ix, h_ix, selc]
        s_sel = jnp.einsum('bhqd,bhqnkd->bhqnk', qc, k_sel).astype(f32) * scale
        s_sel = jnp.where(okc[..., None], s_sel, -jnp.inf)
        s_own = jnp.einsum('bhqd,bhkd->bhqk', qc, k_own).astype(f32) * scale
        qpos = q0 + jnp.arange(Q_CHUNK)
        kpos = own * MOBA_BLOCK + jnp.arange(MOBA_BLOCK)
        s_own = jnp.where(kpos[None, :] <= qpos[:, None], s_own, -jnp.inf)
        sc = jnp.concatenate([s_sel.reshape(bsz, nh, Q_CHUNK, n_sel_keys), s_own], axis=-1)
        p = jax.nn.softmax(sc, axis=-1).astype(v.dtype)
        p_sel = p[..., :n_sel_keys].reshape(bsz, nh, Q_CHUNK, n_sel, MOBA_BLOCK)
        p_own = p[..., n_sel_keys:]
        return (jnp.einsum('bhqnk,bhqnkd->bhqd', p_sel, v_sel)
                + jnp.einsum('bhqk,bhkd->bhqd', p_own, v_own))

    out = lax.map(chunk, (jnp.arange(n_chunks), to_chunks(q), to_chunks(sel), to_chunks(sel_ok)))
    return jnp.moveaxis(out, 0, 2).reshape(bsz, nh, s, dh)


def moe_ffn(h, w_router, b_router, w_gu, b_gu, w_down, b_down):
    bsz, s, dm = h.shape
    n = bsz * s
    hf = h.reshape(n, dm)
    logits = (hf @ w_router + b_router).astype(jnp.float32)
    top_vals, top_idx = lax.top_k(logits, TOP_K)
    top_w = jax.nn.softmax(top_vals, axis=-1)
    nk = n * TOP_K
    flat_e = top_idx.reshape(nk)
    flat_t = jnp.arange(nk, dtype=jnp.int32) // TOP_K
    flat_w = top_w.reshape(nk)
    order = jnp.argsort(flat_e)
    sorted_e = flat_e[order]
    counts = jnp.bincount(flat_e, length=N_EXPERTS)
    group_start = jnp.cumsum(counts) - counts
    padded = (counts + EXPERT_ROWS - 1) // EXPERT_ROWS * EXPERT_ROWS
    padded_end = jnp.cumsum(padded)
    padded_start = padded_end - padded
    n_blocks = -(-nk // EXPERT_ROWS) + N_EXPERTS
    rows_total = n_blocks * EXPERT_ROWS
    dest = padded_start[sorted_e] + jnp.arange(nk, dtype=jnp.int32) - group_start[sorted_e]
    row_tok = jnp.full((rows_total,), n, dtype=jnp.int32).at[dest].set(flat_t[order])
    row_w = jnp.zeros((rows_total,), jnp.float32).at[dest].set(flat_w[order])
    blk_e = jnp.minimum(jnp.searchsorted(padded_end, jnp.arange(n_blocks) * EXPERT_ROWS, side='right'),
                        N_EXPERTS - 1)
    h_pad = jnp.concatenate([hf, jnp.zeros((1, dm), hf.dtype)], axis=0)

    def expert_block(args):
        e, tok, wt = args
        xb = h_pad[tok]
        gu = xb @ w_gu[e] + b_gu[e]
        gate = jnp.minimum(gu[:, :D_FF], SWIGLU_LIMIT)
        up = jnp.clip(gu[:, D_FF:], -SWIGLU_LIMIT, SWIGLU_LIMIT)
        act = (up + 1.0) * gate * jax.nn.sigmoid(SWIGLU_ALPHA * gate)
        y = act @ w_down[e] + b_down[e]
        return y * wt[:, None].astype(y.dtype)

    rows = lax.map(expert_block, (blk_e, row_tok.reshape(n_blocks, EXPERT_ROWS),
                                  row_w.reshape(n_blocks, EXPERT_ROWS)))
    out = jnp.zeros((n + 1, dm), rows.dtype).at[row_tok].add(rows.reshape(rows_total, dm))[:n]
    return out.reshape(bsz, s, dm).astype(h.dtype)


def setup_inputs(seed: int = 0) -> dict:
    key = jax.random.key(seed)
    ks = jax.random.split(key, 32)
    f32 = jnp.float32
    L, D, G, P, C = DEPTH, D_MODEL, SSM_GROUPS, SSM_STATE, SSM_GROUP

    def nrm(k, shape, scale):
        return jax.random.normal(k, shape, f32) * scale

    u01 = jax.random.uniform(ks[8], (L, G), f32)
    return {
        'x': nrm(ks[0], (BATCH, SEQ, D), 1.0),
        'c': nrm(ks[1], (BATCH, D), 1.0),
        'w_ada': nrm(ks[2], (L, D, 6 * D), D ** -0.5),
        'b_ada': nrm(ks[3], (L, 6 * D), 0.01),
        'g_mix': 1.0 + nrm(ks[4], (L, D), 0.01),
        'w_in': nrm(ks[5], (L, D, IN_WIDTH), D ** -0.5),
        'ssm_a_re': -0.5 + nrm(ks[6], (L, G, P), 0.01),
        'ssm_a_im': math.pi * jnp.arange(P, dtype=f32) + nrm(ks[7], (L, G, P), 0.01),
        'ssm_log_dt': math.log(DT_MIN) + u01 * (math.log(DT_MAX) - math.log(DT_MIN)),
        'ssm_b_re': nrm(ks[9], (L, G, P, C), (2 * C) ** -0.5),
        'ssm_b_im': nrm(ks[10], (L, G, P, C), (2 * C) ** -0.5),
        'ssm_c_re': nrm(ks[11], (L, G, C, P), (2 * P) ** -0.5),
        'ssm_c_im': nrm(ks[12], (L, G, C, P), (2 * P) ** -0.5),
        'ssm_d': nrm(ks[13], (L, SSM_WIDTH), 1.0),
        'w_glu': nrm(ks[14], (L, SSM_WIDTH, 2 * D), SSM_WIDTH ** -0.5),
        'w_attn_o': nrm(ks[15], (L, ATTN_WIDTH, D), ATTN_WIDTH ** -0.5),
        'w_out': nrm(ks[16], (L, D, D), D ** -0.5),
        'g_ffn': 1.0 + nrm(ks[17], (L, D), 0.01),
        'w_router': nrm(ks[18], (L, D, N_EXPERTS), D ** -0.5),
        'b_router': nrm(ks[19], (L, N_EXPERTS), 0.01),
        'w_gu': nrm(ks[20], (L, N_EXPERTS, D, 2 * D_FF), D ** -0.5),
        'b_gu': nrm(ks[21], (L, N_EXPERTS, 2 * D_FF), 0.01),
        'w_down': nrm(ks[22], (L, N_EXPERTS, D_FF, D), D_FF ** -0.5),
        'b_down': nrm(ks[23], (L, N_EXPERTS, D), 0.01),
        'g_final': 1.0 + nrm(ks[24], (D,), 0.01),
    }


def reference(x, c, w_ada, b_ada, g_mix, w_in, ssm_a_re, ssm_a_im, ssm_log_dt, ssm_b_re, ssm_b_im,
              ssm_c_re, ssm_c_im, ssm_d, w_glu, w_attn_o, w_out, g_ffn, w_router, b_router,
              w_gu, b_gu, w_down, b_down, g_final):
    s = x.shape[1]
    cos, sin = rotary_tables(s)
    cond = jax.nn.silu(c)
    splits = [SSM_WIDTH, SSM_WIDTH + ATTN_WIDTH, SSM_WIDTH + 2 * ATTN_WIDTH,
              SSM_WIDTH + 3 * ATTN_WIDTH, SSM_WIDTH + 3 * ATTN_WIDTH + D_MODEL]
    for l in range(DEPTH):
        mod = cond @ w_ada[l] + b_ada[l]
        sh1, sc1, gt1, sh2, sc2, gt2 = jnp.split(mod, 6, axis=-1)
        h = modulate(rms_norm(x, g_mix[l]), sh1, sc1)
        proj = h @ w_in[l]
        u, q, k, v, ga, gb = jnp.split(proj, splits, axis=-1)
        y_ssm = s5_mixer(u, ssm_a_re[l], ssm_a_im[l], ssm_log_dt[l], ssm_b_re[l], ssm_b_im[l],
                         ssm_c_re[l], ssm_c_im[l], ssm_d[l])
        glu = jax.nn.gelu(y_ssm) @ w_glu[l]
        y_a = glu[..., :D_MODEL] * jax.nn.sigmoid(glu[..., D_MODEL:])
        qh = partial_rotary(to_heads(q), cos, sin)
        kh = partial_rotary(to_heads(k), cos, sin)
        o = moba_attention(qh, kh, to_heads(v))
        o = o.transpose(0, 2, 1, 3).reshape(x.shape[0], s, ATTN_WIDTH)
        y_b = o @ w_attn_o[l]
        merged = jax.nn.sigmoid(ga) * y_a + jax.nn.sigmoid(gb) * y_b
        x = x + gt1[:, None, :] * (merged @ w_out[l])
        h2 = modulate(rms_norm(x, g_ffn[l]), sh2, sc2)
        x = x + gt2[:, None, :] * moe_ffn(h2, w_router[l], b_router[l], w_gu[l], b_gu[l],
                                          w_down[l], b_down[l])
    return rms_norm(x, g_final)
```

```python
import functools
import math

import jax
import jax.numpy as jnp
from jax import lax
from jax.experimental import pallas as pl
from jax.experimental.pallas import tpu as pltpu

F32 = jnp.float32
BF16 = jnp.bfloat16

SSM_GROUP = 16
SSM_CHUNK = 16
HEAD_DIM = 64
MOBA_BLOCK = 256
MOBA_TOPK = 3
ROPE_THETA = 500000.0
ROT_DIM = HEAD_DIM // 4
TOP_K = 4
SWIGLU_LIMIT = 7.0
SWIGLU_ALPHA = 1.702
EXPERT_ROWS = 256
RMS_EPS = 1e-5
LANES = 128
SUBLANES = 8
NEG = -1e30
VMEM_LIMIT = 56 * 1024 * 1024


def _cparams(**kw):
    return pltpu.CompilerParams(vmem_limit_bytes=VMEM_LIMIT, **kw)


def _const_spec(shape):
    nd = len(shape)
    return pl.BlockSpec(shape, lambda *_: (0,) * nd, pipeline_mode=pl.Buffered(1))


def _ada_kernel(c_ref, w_ref, b_ref, o_ref):
    c = c_ref[...]
    cond = c * jax.nn.sigmoid(c)
    o_ref[...] = jnp.dot(cond, w_ref[...], preferred_element_type=F32) + b_ref[...]


def ada_mod(c, w, b):
    bsz, d = c.shape
    n = w.shape[1]
    tn = min(n, 1024)
    return pl.pallas_call(
        _ada_kernel,
        grid=(n // tn,),
        in_specs=[pl.BlockSpec((bsz, d), lambda j: (0, 0)),
                  pl.BlockSpec((d, tn), lambda j: (0, j)),
                  pl.BlockSpec((1, tn), lambda j: (0, j))],
        out_specs=pl.BlockSpec((bsz, tn), lambda j: (0, j)),
        out_shape=jax.ShapeDtypeStruct((bsz, n), F32),
        compiler_params=_cparams(),
        name="ada_mod",
    )(c, w, b[None])


def _rope_tables(seq):
    half = ROT_DIM // 2
    inv_freq = 1.0 / (ROPE_THETA ** (jnp.arange(0, ROT_DIM, 2, dtype=F32) / ROT_DIM))
    ang = jnp.arange(seq, dtype=F32)[:, None] * inv_freq[None, :]
    cos, sin = jnp.cos(ang), jnp.sin(ang)
    d = jnp.arange(LANES) % HEAD_DIM
    f = d % half
    cos_l = jnp.where(d < ROT_DIM, cos[:, f], 1.0)
    sin_a = jnp.where(d < half, -sin[:, f], 0.0)
    sin_b = jnp.where((d >= half) & (d < ROT_DIM), sin[:, f], 0.0)
    return cos_l, sin_a, sin_b, cos.T, sin.T


def _inproj_kernel(x_ref, sh_ref, sc_ref, g_ref, wm_ref, wt_ref, cl_ref, sa_ref, sb_ref, ct_ref, st_ref,
                   u_ref, k_ref, km_ref, ga_ref, gb_ref, qt_ref, vt_ref, *, sw, aw):
    tm, d = x_ref.shape[1], x_ref.shape[2]
    half = ROT_DIM // 2
    xf = x_ref[0]
    r = lax.rsqrt(jnp.mean(xf * xf, axis=-1, keepdims=True) + RMS_EPS)
    h = (xf * r * g_ref[...]) * (1.0 + sc_ref[0]) + sh_ref[0]
    hb = h.astype(BF16)
    pm = jnp.dot(hb, wm_ref[...], preferred_element_type=F32)
    u_ref[0] = pm[:, :sw].astype(BF16)
    ga_ref[0] = jax.nn.sigmoid(pm[:, sw + aw:sw + aw + d]).astype(BF16)
    gb_ref[0] = jax.nn.sigmoid(pm[:, sw + aw + d:]).astype(BF16)

    cl, sa, sb = cl_ref[...], sa_ref[...], sb_ref[...]
    parts = []
    for t in range(aw // LANES):
        kt = pm[:, sw + t * LANES:sw + (t + 1) * LANES]
        parts.append(kt * cl + pltpu.roll(kt, LANES - half, 1) * sa + pltpu.roll(kt, half, 1) * sb)
    krot = jnp.concatenate(parts, axis=1)
    k_ref[0] = krot.astype(BF16)
    km_ref[0, 0] = jnp.mean(krot.reshape(tm // MOBA_BLOCK, MOBA_BLOCK, aw), axis=1)

    qv = lax.dot_general(wt_ref[...], hb, (((1,), (1,)), ((), ())), preferred_element_type=F32)
    ct, st = ct_ref[...], st_ref[...]
    rows = []
    for hd in range(aw // HEAD_DIM):
        base = hd * HEAD_DIM
        t1 = qv[base:base + half]
        t2 = qv[base + half:base + ROT_DIM]
        rows += [t1 * ct - t2 * st, t2 * ct + t1 * st, qv[base + ROT_DIM:base + HEAD_DIM]]
    qb = (jnp.concatenate(rows, axis=0) * (HEAD_DIM ** -0.5)).astype(BF16)
    vb = qv[aw:].astype(BF16)
    for cb in range(tm // MOBA_BLOCK):
        qt_ref[0, cb] = qb[:, cb * MOBA_BLOCK:(cb + 1) * MOBA_BLOCK]
        vt_ref[0, cb] = vb[:, cb * MOBA_BLOCK:(cb + 1) * MOBA_BLOCK]


def in_proj(x, shift, scale, g, w_in, sw, aw, tm=512):
    bsz, seq, d = x.shape
    tm = min(tm, seq)
    nb = seq // MOBA_BLOCK
    wm = jnp.concatenate([w_in[:, :sw], w_in[:, sw + aw:sw + 2 * aw], w_in[:, sw + 3 * aw:]], axis=1).astype(BF16)
    wt = jnp.concatenate([w_in[:, sw:sw + aw], w_in[:, sw + 2 * aw:sw + 3 * aw]], axis=1).T.astype(BF16)
    cos_l, sin_a, sin_b, cos_t, sin_t = _rope_tables(seq)
    tok = lambda w: pl.BlockSpec((1, tm, w), lambda b, i: (b, i, 0))
    vec = pl.BlockSpec((1, 1, d), lambda b, i: (b, 0, 0))
    tab = pl.BlockSpec((tm, LANES), lambda b, i: (i, 0))
    tabt = pl.BlockSpec((ROT_DIM // 2, tm), lambda b, i: (0, i))
    blkt = pl.BlockSpec((1, tm // MOBA_BLOCK, aw, MOBA_BLOCK), lambda b, i: (b, i, 0, 0))
    return pl.pallas_call(
        functools.partial(_inproj_kernel, sw=sw, aw=aw),
        grid=(bsz, seq // tm),
        in_specs=[tok(d), vec, vec, _const_spec((1, d)), _const_spec(wm.shape), _const_spec(wt.shape),
                  tab, tab, tab, tabt, tabt],
        out_specs=[tok(sw), tok(aw),
                   pl.BlockSpec((1, 1, tm // MOBA_BLOCK, aw), lambda b, i: (b, i, 0, 0)),
                   tok(d), tok(d), blkt, blkt],
        out_shape=[jax.ShapeDtypeStruct((bsz, seq, sw), BF16),
                   jax.ShapeDtypeStruct((bsz, seq, aw), BF16),
                   jax.ShapeDtypeStruct((bsz, seq // tm, tm // MOBA_BLOCK, aw), F32),
                   jax.ShapeDtypeStruct((bsz, seq, d), BF16),
                   jax.ShapeDtypeStruct((bsz, seq, d), BF16),
                   jax.ShapeDtypeStruct((bsz, nb, aw, MOBA_BLOCK), BF16),
                   jax.ShapeDtypeStruct((bsz, nb, aw, MOBA_BLOCK), BF16)],
        compiler_params=_cparams(),
        name="in_proj",
    )(x, shift[:, None], scale[:, None], g[None], wm, wt, cos_l, sin_a, sin_b, cos_t, sin_t)


def _attn_kernel(q_ref, k_ref, v_ref, km_ref, o_ref, bias_ref):
    i = pl.program_id(2)
    nb = km_ref.shape[1]
    blk = MOBA_BLOCK
    q2 = q_ref[0, 0]
    rowid = lax.broadcasted_iota(jnp.int32, (2 * HEAD_DIM, 1), 0)
    kmb = km_ref[0].astype(BF16)
    bid = lax.broadcasted_iota(jnp.int32, (nb, 1), 0)
    qs = []
    for hh in range(2):
        in_head = (rowid >= hh * HEAD_DIM) & (rowid < (hh + 1) * HEAD_DIM)
        qh = jnp.where(in_head, q2, jnp.zeros_like(q2))
        qs.append(qh)
        gate = jnp.dot(kmb, qh, preferred_element_type=F32)
        g = jnp.where(bid < i, gate, -jnp.inf)
        bias = jnp.full(g.shape, NEG, F32)
        for _ in range(MOBA_TOPK):
            m = jnp.max(g, axis=0, keepdims=True)
            first = jnp.min(jnp.where(g == m, bid, nb), axis=0, keepdims=True)
            hit = (bid == first) & (m > -jnp.inf)
            bias = jnp.where(hit, 0.0, bias)
            g = jnp.where(hit, -jnp.inf, g)
        bias_ref[hh] = bias

    kpos = lax.broadcasted_iota(jnp.int32, (blk, blk), 0)
    qpos = lax.broadcasted_iota(jnp.int32, (blk, blk), 1)
    causal = kpos <= qpos
    kb = k_ref[0, pl.ds(pl.multiple_of(i * blk, blk), blk), :]
    vb = v_ref[0, i]
    state = []
    for hh in range(2):
        s = jnp.dot(kb, qs[hh], preferred_element_type=F32)
        s = jnp.where(causal, s, NEG)
        m = jnp.max(s, axis=0, keepdims=True)
        p = jnp.exp(s - m)
        l = jnp.sum(p, axis=0, keepdims=True)
        acc = jnp.dot(vb[hh * HEAD_DIM:(hh + 1) * HEAD_DIM], p.astype(BF16), preferred_element_type=F32)
        state += [m, l, acc]

    def body(j, st):
        kb = k_ref[0, pl.ds(pl.multiple_of(j * blk, blk), blk), :]
        vb = v_ref[0, j]
        out = []
        for hh in range(2):
            m, l, acc = st[3 * hh:3 * hh + 3]
            s = jnp.dot(kb, qs[hh], preferred_element_type=F32) + bias_ref[hh, pl.ds(j, 1), :]
            mn = jnp.maximum(m, jnp.max(s, axis=0, keepdims=True))
            a = jnp.exp(m - mn)
            p = jnp.exp(s - mn)
            l = a * l + jnp.sum(p, axis=0, keepdims=True)
            acc = a * acc + jnp.dot(vb[hh * HEAD_DIM:(hh + 1) * HEAD_DIM], p.astype(BF16),
                                    preferred_element_type=F32)
            out += [mn, l, acc]
        return tuple(out)

    st = lax.fori_loop(0, i, body, tuple(state))
    for hh in range(2):
        _, l, acc = st[3 * hh:3 * hh + 3]
        o_ref[0, 0, hh * HEAD_DIM:(hh + 1) * HEAD_DIM, :] = (acc / l).astype(BF16)


def moba_attn(qt, k, vt, kmean):
    bsz, nb, aw, blk = qt.shape
    seq = k.shape[1]
    pair = 2 * HEAD_DIM
    return pl.pallas_call(
        _attn_kernel,
        grid=(bsz, aw // pair, nb),
        in_specs=[pl.BlockSpec((1, 1, pair, blk), lambda b, p, i: (b, i, p, 0)),
                  pl.BlockSpec((1, seq, pair), lambda b, p, i: (b, 0, p)),
                  pl.BlockSpec((1, nb, pair, blk), lambda b, p, i: (b, 0, p, 0)),
                  pl.BlockSpec((1, nb, pair), lambda b, p, i: (b, 0, p))],
        out_specs=pl.BlockSpec((1, 1, pair, blk), lambda b, p, i: (b, i, p, 0)),
        out_shape=jax.ShapeDtypeStruct((bsz, nb, aw, blk), BF16),
        scratch_shapes=[pltpu.VMEM((2, nb, blk), F32)],
        compiler_params=_cparams(),
        name="moba_attn",
    )(qt, k, vt, kmean)


def _ssm_matrices(a_re, a_im, log_dt, b_re, b_im, c_re, c_im):
    hi = lax.Precision.HIGHEST
    t_len = SSM_CHUNK
    grp, p_st = a_re.shape
    ch = b_re.shape[-1]
    dt = jnp.exp(log_dt.astype(F32))[:, None]
    lr, li = a_re.astype(F32), a_im.astype(F32)
    n = jnp.arange(t_len + 1, dtype=F32)[:, None, None]
    mag = jnp.exp(lr * dt * n)
    pw_r, pw_i = mag * jnp.cos(li * dt * n), mag * jnp.sin(li * dt * n)
    abar_r, abar_i = pw_r[1], pw_i[1]
    den = lr * lr + li * li
    coef_r = ((abar_r - 1.0) * lr + abar_i * li) / den
    coef_i = (abar_i * lr - (abar_r - 1.0) * li) / den
    br, bi = b_re.astype(F32), b_im.astype(F32)
    bbar_r = coef_r[..., None] * br - coef_i[..., None] * bi
    bbar_i = coef_r[..., None] * bi + coef_i[..., None] * br
    cr, ci = c_re.astype(F32), c_im.astype(F32)
    cp_r = cr[None] * pw_r[:, :, None, :] - ci[None] * pw_i[:, :, None, :]
    cp_i = cr[None] * pw_i[:, :, None, :] + ci[None] * pw_r[:, :, None, :]
    k_lag = (jnp.einsum('tgcp,gpd->tgcd', cp_r[:t_len], bbar_r, precision=hi)
             - jnp.einsum('tgcp,gpd->tgcd', cp_i[:t_len], bbar_i, precision=hi))
    lag = jnp.arange(t_len)[None, :] - jnp.arange(t_len)[:, None]
    kfull = jnp.where((lag >= 0)[:, :, None, None, None], k_lag[jnp.maximum(lag, 0)], 0.0)
    kmat = kfull.transpose(2, 0, 4, 1, 3).reshape(grp, t_len * ch, t_len * ch)
    rev = t_len - 1 - jnp.arange(t_len)
    bm_r = pw_r[rev][..., None] * bbar_r[None] - pw_i[rev][..., None] * bbar_i[None]
    bm_i = pw_r[rev][..., None] * bbar_i[None] + pw_i[rev][..., None] * bbar_r[None]
    bm_r = bm_r.transpose(1, 0, 3, 2).reshape(grp, t_len * ch, p_st)
    bm_i = bm_i.transpose(1, 0, 3, 2).reshape(grp, t_len * ch, p_st)
    cm_r = cp_r[1:].transpose(1, 3, 0, 2).reshape(grp, p_st, t_len * ch)
    cm_i = (-cp_i[1:]).transpose(1, 3, 0, 2).reshape(grp, p_st, t_len * ch)
    return kmat, bm_r, bm_i, cm_r, cm_i, pw_r[t_len], pw_i[t_len]


def _pair_pad(m, axis):
    z = jnp.zeros_like(m)
    even = jnp.concatenate([m, z], axis=axis)
    odd = jnp.concatenate([z, m], axis=axis)
    sel = (jnp.arange(m.shape[0]) % 2 == 0).reshape((-1,) + (1,) * (m.ndim - 1))
    return jnp.where(sel, even, odd)


def _ssm_state_kernel(u_ref, br_ref, bi_ref, sr_ref, si_ref):
    u0, u1 = u_ref[0, 0], u_ref[0, 1]
    sr_ref[...] = (jnp.dot(u0, br_ref[0], preferred_element_type=F32)
                   + jnp.dot(u1, br_ref[1], preferred_element_type=F32))
    si_ref[...] = (jnp.dot(u0, bi_ref[0], preferred_element_type=F32)
                   + jnp.dot(u1, bi_ref[1], preferred_element_type=F32))


def _ssm_scan_kernel(sr_ref, si_ref, ar_ref, ai_ref, xr_ref, xi_ref):
    nc = sr_ref.shape[0]
    ar, ai = ar_ref[...], ai_ref[...]

    def body(c, carry):
        xr, xi = carry
        xr_ref[c] = xr
        xi_ref[c] = xi
        return (ar * xr - ai * xi + sr_ref[c], ar * xi + ai * xr + si_ref[c])

    zero = jnp.zeros(ar.shape, F32)
    lax.fori_loop(0, nc, body, (zero, zero), unroll=8)


def _ssm_out_kernel(u_ref, km_ref, xr_ref, xi_ref, cr_ref, ci_ref, y_ref):
    xr = xr_ref[...].astype(BF16)
    xi = xi_ref[...].astype(BF16)
    for hh in range(2):
        y_ref[0, hh] = (jnp.dot(u_ref[0, hh], km_ref[hh], preferred_element_type=F32)
                        + jnp.dot(xr, cr_ref[hh], preferred_element_type=F32)
                        + jnp.dot(xi, ci_ref[hh], preferred_element_type=F32))


def ssm_mixer(u, a_re, a_im, log_dt, b_re, b_im, c_re, c_im):
    bsz, seq, sw = u.shape
    grp = sw // SSM_GROUP
    t_len = SSM_CHUNK
    nc = seq // t_len
    kw = t_len * SSM_GROUP
    kmat, bm_r, bm_i, cm_r, cm_i, at_r, at_i = _ssm_matrices(a_re, a_im, log_dt, b_re, b_im, c_re, c_im)
    p_st = at_r.shape[1]
    kmat = kmat.astype(BF16)
    bm_r, bm_i = _pair_pad(bm_r, 2).astype(BF16), _pair_pad(bm_i, 2).astype(BF16)
    cm_r, cm_i = _pair_pad(cm_r, 1).astype(BF16), _pair_pad(cm_i, 1).astype(BF16)
    ug = u.reshape(bsz, nc, t_len, grp, SSM_GROUP).transpose(0, 3, 1, 2, 4).reshape(bsz, grp, nc, kw)
    gp = grp // 2
    ncols = bsz * gp * 2 * p_st
    col = lambda b, g: (0, b * gp + g)
    u_spec = pl.BlockSpec((1, 2, nc, kw), lambda b, g: (b, g, 0, 0))
    s_spec = pl.BlockSpec((nc, 2 * p_st), col)
    s_re, s_im = pl.pallas_call(
        _ssm_state_kernel,
        grid=(bsz, gp),
        in_specs=[u_spec,
                  pl.BlockSpec((2, kw, 2 * p_st), lambda b, g: (g, 0, 0)),
                  pl.BlockSpec((2, kw, 2 * p_st), lambda b, g: (g, 0, 0))],
        out_specs=[s_spec, s_spec],
        out_shape=[jax.ShapeDtypeStruct((nc, ncols), F32)] * 2,
        compiler_params=_cparams(),
        name="ssm_state",
    )(ug, bm_r, bm_i)

    sub = SUBLANES
    lw = ncols // sub
    a_r = jnp.tile(at_r.reshape(-1), bsz).reshape(sub, lw)
    a_i = jnp.tile(at_i.reshape(-1), bsz).reshape(sub, lw)
    sc_spec = pl.BlockSpec((nc, sub, LANES), lambda j: (0, 0, j))
    a_spec = pl.BlockSpec((sub, LANES), lambda j: (0, j))
    x_re, x_im = pl.pallas_call(
        _ssm_scan_kernel,
        grid=(lw // LANES,),
        in_specs=[sc_spec, sc_spec, a_spec, a_spec],
        out_specs=[sc_spec, sc_spec],
        out_shape=[jax.ShapeDtypeStruct((nc, sub, lw), F32)] * 2,
        compiler_params=_cparams(),
        name="ssm_scan",
    )(s_re.reshape(nc, sub, lw), s_im.reshape(nc, sub, lw), a_r, a_i)

    y = pl.pallas_call(
        _ssm_out_kernel,
        grid=(bsz, gp),
        in_specs=[u_spec,
                  pl.BlockSpec((2, kw, kw), lambda b, g: (g, 0, 0)),
                  s_spec, s_spec,
                  pl.BlockSpec((2, 2 * p_st, kw), lambda b, g: (g, 0, 0)),
                  pl.BlockSpec((2, 2 * p_st, kw), lambda b, g: (g, 0, 0))],
        out_specs=pl.BlockSpec((1, 2, nc, kw), lambda b, g: (b, g, 0, 0)),
        out_shape=jax.ShapeDtypeStruct((bsz, grp, nc, kw), F32),
        compiler_params=_cparams(),
        name="ssm_out",
    )(ug, kmat, x_re.reshape(nc, ncols), x_im.reshape(nc, ncols), cm_r, cm_i)
    return y.reshape(bsz, grp, nc, t_len, SSM_GROUP).transpose(0, 2, 3, 1, 4).reshape(bsz, seq, sw)


def _merge_kernel(x_ref, y_ref, u_ref, dsk_ref, ot_ref, ga_ref, gb_ref, wglu_ref, wo_ref, wout_ref,
                  gt_ref, sh_ref, sc_ref, g_ref, wr_ref, br_ref,
                  x1_ref, h2_ref, ti_ref, tw_ref):
    d = x_ref.shape[2]
    n_exp = wr_ref.shape[0]
    y = y_ref[0] + dsk_ref[...] * u_ref[0].astype(F32)
    a = jax.nn.gelu(y, approximate=True).astype(BF16)
    glu = jnp.dot(a, wglu_ref[...], preferred_element_type=F32)
    y_a = glu[:, :d] * jax.nn.sigmoid(glu[:, d:])
    y_b = jnp.concatenate(
        [lax.dot_general(ot_ref[0, cb], wo_ref[...], (((0,), (0,)), ((), ())), preferred_element_type=F32)
         for cb in range(ot_ref.shape[1])], axis=0)
    merged = ga_ref[0].astype(F32) * y_a + gb_ref[0].astype(F32) * y_b
    z = jnp.dot(merged.astype(BF16), wout_ref[...], preferred_element_type=F32)
    x1 = x_ref[0] + gt_ref[0] * z
    x1_ref[0] = x1
    r = lax.rsqrt(jnp.mean(x1 * x1, axis=-1, keepdims=True) + RMS_EPS)
    h2 = (x1 * r * g_ref[...]) * (1.0 + sc_ref[0]) + sh_ref[0]
    h2_ref[0] = h2
    logits = lax.dot_general(wr_ref[...], h2.astype(BF16), (((1,), (1,)), ((), ())),
                             preferred_element_type=F32) + br_ref[...]
    eid = lax.broadcasted_iota(jnp.int32, (n_exp, 1), 0)
    vals, idxs = [], []
    g = logits
    for _ in range(TOP_K):
        m = jnp.max(g, axis=0, keepdims=True)
        first = jnp.min(jnp.where(g == m, eid, n_exp), axis=0, keepdims=True)
        vals.append(m)
        idxs.append(first)
        g = jnp.where(eid == first, -jnp.inf, g)
    v = jnp.concatenate(vals, axis=0)
    e = jnp.exp(v - v[0:1])
    tw_ref[0] = e / jnp.sum(e, axis=0, keepdims=True)
    ti_ref[0] = jnp.concatenate(idxs, axis=0)


def merge(x, y_ssm, u, d_skip, ot, ga, gb, w_glu, w_attn_o, w_out, gate, shift, scale, g, w_router, b_router,
          tm=256):
    bsz, seq, d = x.shape
    sw = u.shape[2]
    aw = ot.shape[2]
    n_exp = w_router.shape[1]
    tm = min(tm, seq)
    tok = lambda w: pl.BlockSpec((1, tm, w), lambda b, i: (b, i, 0))
    vec = pl.BlockSpec((1, 1, d), lambda b, i: (b, 0, 0))
    sel = pl.BlockSpec((1, TOP_K, tm), lambda b, i: (b, 0, i))
    return pl.pallas_call(
        _merge_kernel,
        grid=(bsz, seq // tm),
        in_specs=[tok(d), tok(sw), tok(sw), _const_spec((1, sw)),
                  pl.BlockSpec((1, tm // MOBA_BLOCK, aw, MOBA_BLOCK), lambda b, i: (b, i, 0, 0)),
                  tok(d), tok(d),
                  _const_spec(w_glu.shape), _const_spec(w_attn_o.shape), _const_spec(w_out.shape),
                  vec, vec, vec, _const_spec((1, d)), _const_spec((n_exp, d)), _const_spec((n_exp, 1))],
        out_specs=[tok(d), tok(d), sel, sel],
        out_shape=[jax.ShapeDtypeStruct((bsz, seq, d), F32),
                   jax.ShapeDtypeStruct((bsz, seq, d), F32),
                   jax.ShapeDtypeStruct((bsz, TOP_K, seq), jnp.int32),
                   jax.ShapeDtypeStruct((bsz, TOP_K, seq), F32)],
        compiler_params=_cparams(),
        name="merge",
    )(x, y_ssm, u, d_skip[None], ot, ga, gb, w_glu.astype(BF16), w_attn_o.astype(BF16), w_out.astype(BF16),
      gate[:, None], shift[:, None], scale[:, None], g[None], w_router.T.astype(BF16), b_router[:, None])


def _moe_kernel(blk_e_ref, nvalid_ref, tok_hbm, pair_hbm, w_ref, h_hbm, wgu_ref, bgu_ref, wd_ref, bd_ref,
                out_hbm, tok_s, pair_s, xbuf, ybuf, sem_idx, sem_g, sem_s):
    i = pl.program_id(0)
    rows = xbuf.shape[0]
    f = wd_ref.shape[1]
    nv = nvalid_ref[i]

    @pl.when(nv > 0)
    def _():
        cp_t = pltpu.make_async_copy(tok_hbm.at[i], tok_s, sem_idx.at[0])
        cp_p = pltpu.make_async_copy(pair_hbm.at[i], pair_s, sem_idx.at[1])
        cp_t.start()
        cp_p.start()
        cp_t.wait()
        cp_p.wait()

        def gather(r, carry):
            pltpu.make_async_copy(h_hbm.at[pl.ds(tok_s[r], 1)], xbuf.at[pl.ds(r, 1)], sem_g).start()
            return carry

        lax.fori_loop(0, rows, gather, 0)
        pltpu.make_async_copy(h_hbm.at[pl.ds(0, rows)], xbuf, sem_g).wait()

        xb = xbuf[...].astype(BF16)
        gu = jnp.dot(xb, wgu_ref[0], preferred_element_type=F32) + bgu_ref[0]
        gate = jnp.minimum(gu[:, :f], SWIGLU_LIMIT)
        up = jnp.clip(gu[:, f:], -SWIGLU_LIMIT, SWIGLU_LIMIT)
        act = (up + 1.0) * gate * jax.nn.sigmoid(SWIGLU_ALPHA * gate)
        y = jnp.dot(act.astype(BF16), wd_ref[0], preferred_element_type=F32) + bd_ref[0]
        ybuf[...] = y * w_ref[0]

        def scatter(r, carry):
            pltpu.make_async_copy(ybuf.at[pl.ds(r, 1)], out_hbm.at[pl.ds(pair_s[r], 1)], sem_s).start()
            return carry

        lax.fori_loop(0, nv, scatter, 0)

        n_al = pl.multiple_of((nv // SUBLANES) * SUBLANES, SUBLANES)

        @pl.when(n_al > 0)
        def _():
            pltpu.make_async_copy(ybuf.at[pl.ds(0, n_al)], out_hbm.at[pl.ds(0, n_al)], sem_s).wait()

        def wait_row(r, carry):
            pltpu.make_async_copy(ybuf.at[pl.ds(0, 1)], out_hbm.at[pl.ds(0, 1)], sem_s).wait()
            return carry

        lax.fori_loop(n_al, nv, wait_row, 0)


def moe_experts(h2, top_idx, top_w, w_gu, b_gu, w_down, b_down):
    n, d = h2.shape
    n_exp, _, f2 = w_gu.shape
    f = f2 // 2
    rb = EXPERT_ROWS
    nk = n * TOP_K
    flat_e = top_idx.reshape(nk)
    flat_w = top_w.reshape(nk)
    order = jnp.argsort(flat_e)
    sorted_e = flat_e[order]
    counts = jnp.bincount(flat_e, length=n_exp)
    group_start = jnp.cumsum(counts) - counts
    padded = (counts + rb - 1) // rb * rb
    padded_end = jnp.cumsum(padded)
    padded_start = padded_end - padded
    n_blocks = -(-nk // rb) + n_exp
    rows_total = n_blocks * rb
    dest = padded_start[sorted_e] + jnp.arange(nk, dtype=jnp.int32) - group_start[sorted_e]
    row_pair = jnp.zeros((rows_total,), jnp.int32).at[dest].set(order.astype(jnp.int32))
    row_tok = row_pair // TOP_K
    row_w = jnp.zeros((rows_total,), F32).at[dest].set(flat_w[order])
    blk_start = jnp.arange(n_blocks, dtype=jnp.int32) * rb
    blk_e = jnp.minimum(jnp.searchsorted(padded_end, blk_start, side='right'), n_exp - 1).astype(jnp.int32)
    n_valid = jnp.clip(counts[blk_e] - (blk_start - padded_start[blk_e]), 0, rb).astype(jnp.int32)

    grid_spec = pltpu.PrefetchScalarGridSpec(
        num_scalar_prefetch=2,
        grid=(n_blocks,),
        in_specs=[pl.BlockSpec(memory_space=pl.ANY),
                  pl.BlockSpec(memory_space=pl.ANY),
                  pl.BlockSpec((1, rb, 1), lambda i, be, nu: (i, 0, 0)),
                  pl.BlockSpec(memory_space=pl.ANY),
                  pl.BlockSpec((1, d, f2), lambda i, be, nu: (be[i], 0, 0)),
                  pl.BlockSpec((1, 1, f2), lambda i, be, nu: (be[i], 0, 0)),
                  pl.BlockSpec((1, f, d), lambda i, be, nu: (be[i], 0, 0)),
                  pl.BlockSpec((1, 1, d), lambda i, be, nu: (be[i], 0, 0))],
        out_specs=pl.BlockSpec(memory_space=pl.ANY),
        scratch_shapes=[pltpu.SMEM((rb,), jnp.int32), pltpu.SMEM((rb,), jnp.int32),
                        pltpu.VMEM((rb, d), F32), pltpu.VMEM((rb, d), F32),
                        pltpu.SemaphoreType.DMA((2,)), pltpu.SemaphoreType.DMA(()), pltpu.SemaphoreType.DMA(())])
    return pl.pallas_call(
        _moe_kernel,
        grid_spec=grid_spec,
        out_shape=jax.ShapeDtypeStruct((nk, d), F32),
        compiler_params=_cparams(),
        name="moe_experts",
    )(blk_e, n_valid, row_tok.reshape(n_blocks, rb), row_pair.reshape(n_blocks, rb), row_w.reshape(n_blocks, rb, 1),
      h2, w_gu.astype(BF16), b_gu[:, None], w_down.astype(BF16), b_down[:, None])


def _combine_kernel(x_ref, y_ref, gt_ref, g_ref, o_ref):
    d = x_ref.shape[2]
    y = y_ref[...]
    moe = y[:, :d]
    for k in range(1, TOP_K):
        moe = moe + y[:, k * d:(k + 1) * d]
    x2 = x_ref[0] + gt_ref[0] * moe
    r = lax.rsqrt(jnp.mean(x2 * x2, axis=-1, keepdims=True) + RMS_EPS)
    o_ref[0] = x2 * r * g_ref[...]


def combine(x1, ytk, gate, g, tm=256):
    bsz, seq, d = x1.shape
    tm = min(tm, seq)
    nt = seq // tm
    return pl.pallas_call(
        _combine_kernel,
        grid=(bsz, nt),
        in_specs=[pl.BlockSpec((1, tm, d), lambda b, i: (b, i, 0)),
                  pl.BlockSpec((tm, TOP_K * d), lambda b, i: (b * nt + i, 0)),
                  pl.BlockSpec((1, 1, d), lambda b, i: (b, 0, 0)),
                  _const_spec((1, d))],
        out_specs=pl.BlockSpec((1, tm, d), lambda b, i: (b, i, 0)),
        out_shape=jax.ShapeDtypeStruct((bsz, seq, d), F32),
        compiler_params=_cparams(),
        name="combine",
    )(x1, ytk.reshape(ytk.shape[0] // TOP_K, TOP_K * d), gate[:, None], g[None])


def kernel(x, c, w_ada, b_ada, g_mix, w_in, ssm_a_re, ssm_a_im, ssm_log_dt, ssm_b_re, ssm_b_im, ssm_c_re,
           ssm_c_im, ssm_d, w_glu, w_attn_o, w_out, g_ffn, w_router, b_router, w_gu, b_gu, w_down, b_down,
           g_final):
    bsz, seq, d = x.shape
    depth = w_ada.shape[0]
    assert depth == 1
    sw = ssm_d.shape[1]
    aw = w_attn_o.shape[1]
    assert seq % MOBA_BLOCK == 0 and seq % SSM_CHUNK == 0
    for l in range(depth):
        mod = ada_mod(c, w_ada[l], b_ada[l])
        sh1, sc1, gt1, sh2, sc2, gt2 = jnp.split(mod, 6, axis=-1)
        u, k, kmean, ga, gb, qt, vt = in_proj(x, sh1, sc1, g_mix[l], w_in[l], sw, aw)
        ot = moba_attn(qt, k, vt, kmean.reshape(bsz, seq // MOBA_BLOCK, aw))
        y_ssm = ssm_mixer(u, ssm_a_re[l], ssm_a_im[l], ssm_log_dt[l], ssm_b_re[l], ssm_b_im[l],
                          ssm_c_re[l], ssm_c_im[l])
        x1, h2, top_idx, top_w = merge(x, y_ssm, u, ssm_d[l], ot, ga, gb, w_glu[l], w_attn_o[l], w_out[l],
                                       gt1, sh2, sc2, g_ffn[l], w_router[l], b_router[l])
        n = bsz * seq
        ytk = moe_experts(h2.reshape(n, d), top_idx.transpose(0, 2, 1).reshape(n, TOP_K),
                          top_w.transpose(0, 2, 1).reshape(n, TOP_K), w_gu[l], b_gu[l], w_down[l], b_down[l])
        x = combine(x1, ytk, gt2, g_final)
    return x
```

```python
import functools
import math

import jax
import jax.numpy as jnp
from jax import lax
from jax.experimental import pallas as pl
from jax.experimental.pallas import tpu as pltpu

F32 = jnp.float32
BF16 = jnp.bfloat16

SSM_GROUP = 16
SSM_CHUNK = 16
HEAD_DIM = 64
MOBA_BLOCK = 256
MOBA_TOPK = 3
ROPE_THETA = 500000.0
ROT_DIM = HEAD_DIM // 4
QK_SCALE = HEAD_DIM ** -0.5 * math.log2(math.e)
TOP_K = 4
SWIGLU_LIMIT = 7.0
SWIGLU_ALPHA = 1.702
EXPERT_ROWS = 256
RMS_EPS = 1e-5
LANES = 128
SUBLANES = 8
NEG = -1e30
VMEM_LIMIT = 56 * 1024 * 1024


def _cparams(**kw):
    return pltpu.CompilerParams(vmem_limit_bytes=VMEM_LIMIT, **kw)


def _const_spec(shape):
    nd = len(shape)
    return pl.BlockSpec(shape, lambda *_: (0,) * nd, pipeline_mode=pl.Buffered(1))


def _ada_kernel(c_ref, w_ref, b_ref, o_ref):
    c = c_ref[...]
    cond = c * jax.nn.sigmoid(c)
    o_ref[...] = jnp.dot(cond, w_ref[...], preferred_element_type=F32) + b_ref[...]


def ada_mod(c, w, b):
    bsz, d = c.shape
    n = w.shape[1]
    tn = min(n, 1024)
    return pl.pallas_call(
        _ada_kernel,
        grid=(n // tn,),
        in_specs=[pl.BlockSpec((bsz, d), lambda j: (0, 0)),
                  pl.BlockSpec((d, tn), lambda j: (0, j)),
                  pl.BlockSpec((1, tn), lambda j: (0, j))],
        out_specs=pl.BlockSpec((bsz, tn), lambda j: (0, j)),
        out_shape=jax.ShapeDtypeStruct((bsz, n), F32),
        compiler_params=_cparams(),
        name="ada_mod",
    )(c, w, b[None])


def _rope_tables(seq):
    half = ROT_DIM // 2
    inv_freq = 1.0 / (ROPE_THETA ** (jnp.arange(0, ROT_DIM, 2, dtype=F32) / ROT_DIM))
    ang = jnp.arange(seq, dtype=F32)[:, None] * inv_freq[None, :]
    cos, sin = jnp.cos(ang), jnp.sin(ang)
    d = jnp.arange(LANES) % HEAD_DIM
    f = d % half
    cos_l = jnp.where(d < ROT_DIM, cos[:, f], 1.0)
    sin_a = jnp.where(d < half, -sin[:, f], 0.0)
    sin_b = jnp.where((d >= half) & (d < ROT_DIM), sin[:, f], 0.0)
    return cos_l, sin_a, sin_b, cos.T, sin.T


def _inproj_kernel(x_ref, sh_ref, sc_ref, g_ref, wm_ref, wt_ref, cl_ref, sa_ref, sb_ref, ct_ref, st_ref,
                   u_ref, k_ref, km_ref, ga_ref, gb_ref, qt_ref, vt_ref, *, sw, aw):
    tm, d = x_ref.shape[1], x_ref.shape[2]
    half = ROT_DIM // 2
    xf = x_ref[0]
    r = lax.rsqrt(jnp.mean(xf * xf, axis=-1, keepdims=True) + RMS_EPS)
    h = (xf * r * g_ref[...]) * (1.0 + sc_ref[0]) + sh_ref[0]
    hb = h.astype(BF16)
    pm = jnp.dot(hb, wm_ref[...], preferred_element_type=F32)
    u_ref[0] = pm[:, :sw].astype(BF16)
    ga_ref[0] = jax.nn.sigmoid(pm[:, sw + aw:sw + aw + d]).astype(BF16)
    gb_ref[0] = jax.nn.sigmoid(pm[:, sw + aw + d:]).astype(BF16)

    cl, sa, sb = cl_ref[...], sa_ref[...], sb_ref[...]
    parts = []
    for t in range(aw // LANES):
        kt = pm[:, sw + t * LANES:sw + (t + 1) * LANES]
        parts.append(kt * cl + pltpu.roll(kt, LANES - half, 1) * sa + pltpu.roll(kt, half, 1) * sb)
    krot = jnp.concatenate(parts, axis=1)
    k_ref[0] = krot.astype(BF16)
    km_ref[0, 0] = jnp.mean(krot.reshape(tm // MOBA_BLOCK, MOBA_BLOCK, aw), axis=1)

    qv = lax.dot_general(wt_ref[...], hb, (((1,), (1,)), ((), ())), preferred_element_type=F32)
    ct, st = ct_ref[...], st_ref[...]
    rows = []
    for hd in range(aw // HEAD_DIM):
        base = hd * HEAD_DIM
        t1 = qv[base:base + half]
        t2 = qv[base + half:base + ROT_DIM]
        rows += [t1 * ct - t2 * st, t2 * ct + t1 * st, qv[base + ROT_DIM:base + HEAD_DIM]]
    qb = (jnp.concatenate(rows, axis=0) * QK_SCALE).astype(BF16)
    vb = qv[aw:].astype(BF16)
    for cb in range(tm // MOBA_BLOCK):
        qt_ref[0, cb] = qb[:, cb * MOBA_BLOCK:(cb + 1) * MOBA_BLOCK]
        vt_ref[0, cb] = vb[:, cb * MOBA_BLOCK:(cb + 1) * MOBA_BLOCK]


def in_proj(x, shift, scale, g, w_in, sw, aw, tm=512):
    bsz, seq, d = x.shape
    tm = min(tm, seq)
    nb = seq // MOBA_BLOCK
    wm = jnp.concatenate([w_in[:, :sw], w_in[:, sw + aw:sw + 2 * aw], w_in[:, sw + 3 * aw:]], axis=1).astype(BF16)
    wt = jnp.concatenate([w_in[:, sw:sw + aw], w_in[:, sw + 2 * aw:sw + 3 * aw]], axis=1).T.astype(BF16)
    cos_l, sin_a, sin_b, cos_t, sin_t = _rope_tables(seq)
    tok = lambda w: pl.BlockSpec((1, tm, w), lambda b, i: (b, i, 0))
    vec = pl.BlockSpec((1, 1, d), lambda b, i: (b, 0, 0))
    tab = pl.BlockSpec((tm, LANES), lambda b, i: (i, 0))
    tabt = pl.BlockSpec((ROT_DIM // 2, tm), lambda b, i: (0, i))
    blkt = pl.BlockSpec((1, tm // MOBA_BLOCK, aw, MOBA_BLOCK), lambda b, i: (b, i, 0, 0))
    return pl.pallas_call(
        functools.partial(_inproj_kernel, sw=sw, aw=aw),
        grid=(bsz, seq // tm),
        in_specs=[tok(d), vec, vec, _const_spec((1, d)), _const_spec(wm.shape), _const_spec(wt.shape),
                  tab, tab, tab, tabt, tabt],
        out_specs=[tok(sw), tok(aw),
                   pl.BlockSpec((1, 1, tm // MOBA_BLOCK, aw), lambda b, i: (b, i, 0, 0)),
                   tok(d), tok(d), blkt, blkt],
        out_shape=[jax.ShapeDtypeStruct((bsz, seq, sw), BF16),
                   jax.ShapeDtypeStruct((bsz, seq, aw), BF16),
                   jax.ShapeDtypeStruct((bsz, seq // tm, tm // MOBA_BLOCK, aw), F32),
                   jax.ShapeDtypeStruct((bsz, seq, d), BF16),
                   jax.ShapeDtypeStruct((bsz, seq, d), BF16),
                   jax.ShapeDtypeStruct((bsz, nb, aw, MOBA_BLOCK), BF16),
                   jax.ShapeDtypeStruct((bsz, nb, aw, MOBA_BLOCK), BF16)],
        compiler_params=_cparams(),
        name="in_proj",
    )(x, shift[:, None], scale[:, None], g[None], wm, wt, cos_l, sin_a, sin_b, cos_t, sin_t)


ATTN_GROUP = 4
SCORE_ROWS = 32


def _attn_kernel(q_ref, k_ref, v_ref, km_ref, o_ref, qs_ref, bias_ref, s0_ref, s1_ref, p0_ref, p1_ref, acc_ref):
    i = pl.program_id(2)
    nb = km_ref.shape[1]
    blk = MOBA_BLOCK
    gw = ATTN_GROUP * HEAD_DIM
    n_slab = blk // SCORE_ROWS
    q4 = q_ref[0, 0]
    rowid = lax.broadcasted_iota(jnp.int32, (gw, 1), 0)
    kmb = km_ref[0].astype(BF16)
    bid = lax.broadcasted_iota(jnp.int32, (nb, 1), 0)
    for h in range(ATTN_GROUP):
        in_head = (rowid >= h * HEAD_DIM) & (rowid < (h + 1) * HEAD_DIM)
        qh = jnp.where(in_head, q4, jnp.zeros_like(q4))
        qs_ref[h] = qh
        gate = jnp.dot(kmb, qh, preferred_element_type=F32)
        g = jnp.where(bid < i, gate, -jnp.inf)
        bias = jnp.full(g.shape, NEG, F32)
        for _ in range(MOBA_TOPK):
            m = jnp.max(g, axis=0, keepdims=True)
            first = jnp.min(jnp.where(g == m, bid, nb), axis=0, keepdims=True)
            hit = (bid == first) & (m > -jnp.inf)
            bias = jnp.where(hit, 0.0, bias)
            g = jnp.where(hit, -jnp.inf, g)
        bias_ref[h] = bias

    kpos = lax.broadcasted_iota(jnp.int32, (SCORE_ROWS, blk), 0)
    qpos = lax.broadcasted_iota(jnp.int32, (SCORE_ROWS, blk), 1)

    def scores(j, s_ref):
        for h in range(ATTN_GROUP):
            s_ref[h] = jnp.dot(k_ref[0, pl.ds(pl.multiple_of(j * blk, blk), blk), :], qs_ref[h],
                               preferred_element_type=F32)

    def weighted_values(j, p_ref):
        return [jnp.dot(v_ref[0, j, h * HEAD_DIM:(h + 1) * HEAD_DIM, :], p_ref[h], preferred_element_type=F32)
                for h in range(ATTN_GROUP)]

    def block_update(j, s_ref, p_ref, ms, ls, own, nxt=None, pend=None):
        if nxt is not None:
            scores(*nxt)
        if pend is not None:
            pvs = weighted_values(pend[0], pend[1])
        ms_new, ls_new, scales = [], [], []
        for h in range(ATTN_GROUP):
            def slab(c):
                s = s_ref[h, c * SCORE_ROWS:(c + 1) * SCORE_ROWS, :]
                if own:
                    s = jnp.where(kpos + c * SCORE_ROWS <= qpos, s, NEG)
                return s
            cm = slab(0)
            for c in range(1, n_slab):
                cm = jnp.maximum(cm, slab(c))
            cm = jnp.max(cm, axis=0, keepdims=True)
            if own:
                mn, shift = cm, cm
                scales.append(jnp.zeros_like(cm))
            else:
                row = bias_ref[h, pl.ds(j, 1), :]
                mn = jnp.maximum(ms[h], cm + row)
                shift = mn - row
                scales.append(jnp.exp2(ms[h] - mn))
            psum = jnp.zeros((SCORE_ROWS, blk), F32)
            for c in range(n_slab):
                p = jnp.exp2(slab(c) - shift)
                psum = psum + p
                p_ref[h, c * SCORE_ROWS:(c + 1) * SCORE_ROWS, :] = p.astype(BF16)
            lsum = jnp.sum(psum, axis=0, keepdims=True)
            ms_new.append(mn)
            ls_new.append(lsum if own else scales[h] * ls[h] + lsum)
        if pend is not None:
            for h in range(ATTN_GROUP):
                acc_ref[h] = pend[2][h] * acc_ref[h] + pvs[h]
        return ms_new, ls_new, scales

    acc_ref[...] = jnp.zeros(acc_ref.shape, F32)
    scores(i, s1_ref)
    scores(0, s0_ref)
    ms, ls, sc = block_update(i, s1_ref, p1_ref, None, None, True)
    g = ATTN_GROUP

    def body(t, carry):
        j = 2 * t
        prev = jnp.where(t == 0, i, j - 1)
        ms, ls, sc = block_update(j, s0_ref, p0_ref, carry[:g], carry[g:2 * g], False,
                                  nxt=(j + 1, s1_ref), pend=(prev, p1_ref, carry[2 * g:]))
        ms, ls, sc = block_update(j + 1, s1_ref, p1_ref, ms, ls, False,
                                  nxt=(jnp.minimum(j + 2, i), s0_ref), pend=(j, p0_ref, sc))
        return tuple(ms) + tuple(ls) + tuple(sc)

    n_pairs = (i + 1) // 2
    carry = lax.fori_loop(0, n_pairs, body, tuple(ms) + tuple(ls) + tuple(sc))
    last = jnp.where(n_pairs == 0, i, 2 * n_pairs - 1)
    pvs = weighted_values(last, p1_ref)
    for h in range(g):
        acc = carry[2 * g + h] * acc_ref[h] + pvs[h]
        o_ref[0, 0, h * HEAD_DIM:(h + 1) * HEAD_DIM, :] = (acc / carry[g + h]).astype(BF16)


def moba_attn(qt, k, vt, kmean):
    bsz, nb, aw, blk = qt.shape
    seq = k.shape[1]
    gw = ATTN_GROUP * HEAD_DIM
    return pl.pallas_call(
        _attn_kernel,
        grid=(bsz, aw // gw, nb),
        in_specs=[pl.BlockSpec((1, 1, gw, blk), lambda b, p, i: (b, i, p, 0)),
                  pl.BlockSpec((1, seq, gw), lambda b, p, i: (b, 0, p)),
                  pl.BlockSpec((1, nb, gw, blk), lambda b, p, i: (b, 0, p, 0)),
                  pl.BlockSpec((1, nb, gw), lambda b, p, i: (b, 0, p))],
        out_specs=pl.BlockSpec((1, 1, gw, blk), lambda b, p, i: (b, i, p, 0)),
        out_shape=jax.ShapeDtypeStruct((bsz, nb, aw, blk), BF16),
        scratch_shapes=[pltpu.VMEM((ATTN_GROUP, gw, blk), BF16),
                        pltpu.VMEM((ATTN_GROUP, nb, blk), F32),
                        pltpu.VMEM((ATTN_GROUP, blk, blk), F32),
                        pltpu.VMEM((ATTN_GROUP, blk, blk), F32),
                        pltpu.VMEM((ATTN_GROUP, blk, blk), BF16),
                        pltpu.VMEM((ATTN_GROUP, blk, blk), BF16),
                        pltpu.VMEM((ATTN_GROUP, HEAD_DIM, blk), F32)],
        compiler_params=_cparams(),
        name="moba_attn",
    )(qt, k, vt, kmean)


def _ssm_matrices(a_re, a_im, log_dt, b_re, b_im, c_re, c_im):
    hi = lax.Precision.HIGHEST
    t_len = SSM_CHUNK
    grp, p_st = a_re.shape
    ch = b_re.shape[-1]
    dt = jnp.exp(log_dt.astype(F32))[:, None]
    lr, li = a_re.astype(F32), a_im.astype(F32)
    n = jnp.arange(t_len + 1, dtype=F32)[:, None, None]
    mag = jnp.exp(lr * dt * n)
    pw_r, pw_i = mag * jnp.cos(li * dt * n), mag * jnp.sin(li * dt * n)
    abar_r, abar_i = pw_r[1], pw_i[1]
    den = lr * lr + li * li
    coef_r = ((abar_r - 1.0) * lr + abar_i * li) / den
    coef_i = (abar_i * lr - (abar_r - 1.0) * li) / den
    br, bi = b_re.astype(F32), b_im.astype(F32)
    bbar_r = coef_r[..., None] * br - coef_i[..., None] * bi
    bbar_i = coef_r[..., None] * bi + coef_i[..., None] * br
    cr, ci = c_re.astype(F32), c_im.astype(F32)
    cp_r = cr[None] * pw_r[:, :, None, :] - ci[None] * pw_i[:, :, None, :]
    cp_i = cr[None] * pw_i[:, :, None, :] + ci[None] * pw_r[:, :, None, :]
    k_lag = (jnp.einsum('tgcp,gpd->tgcd', cp_r[:t_len], bbar_r, precision=hi)
             - jnp.einsum('tgcp,gpd->tgcd', cp_i[:t_len], bbar_i, precision=hi))
    lag = jnp.arange(t_len)[None, :] - jnp.arange(t_len)[:, None]
    kfull = jnp.where((lag >= 0)[:, :, None, None, None], k_lag[jnp.maximum(lag, 0)], 0.0)
    kmat = kfull.transpose(2, 0, 4, 1, 3).reshape(grp, t_len * ch, t_len * ch)
    rev = t_len - 1 - jnp.arange(t_len)
    bm_r = pw_r[rev][..., None] * bbar_r[None] - pw_i[rev][..., None] * bbar_i[None]
    bm_i = pw_r[rev][..., None] * bbar_i[None] + pw_i[rev][..., None] * bbar_r[None]
    bm_r = bm_r.transpose(1, 0, 3, 2).reshape(grp, t_len * ch, p_st)
    bm_i = bm_i.transpose(1, 0, 3, 2).reshape(grp, t_len * ch, p_st)
    cm_r = cp_r[1:].transpose(1, 3, 0, 2).reshape(grp, p_st, t_len * ch)
    cm_i = (-cp_i[1:]).transpose(1, 3, 0, 2).reshape(grp, p_st, t_len * ch)
    return kmat, bm_r, bm_i, cm_r, cm_i, pw_r[t_len], pw_i[t_len]


def _pair_pad(m, axis):
    z = jnp.zeros_like(m)
    even = jnp.concatenate([m, z], axis=axis)
    odd = jnp.concatenate([z, m], axis=axis)
    sel = (jnp.arange(m.shape[0]) % 2 == 0).reshape((-1,) + (1,) * (m.ndim - 1))
    return jnp.where(sel, even, odd)


def _ssm_state_kernel(u_ref, br_ref, bi_ref, sr_ref, si_ref):
    u0, u1 = u_ref[0, 0], u_ref[0, 1]
    sr_ref[...] = (jnp.dot(u0, br_ref[0], preferred_element_type=F32)
                   + jnp.dot(u1, br_ref[1], preferred_element_type=F32))
    si_ref[...] = (jnp.dot(u0, bi_ref[0], preferred_element_type=F32)
                   + jnp.dot(u1, bi_ref[1], preferred_element_type=F32))


def _ssm_scan_kernel(sr_ref, si_ref, ar_ref, ai_ref, xr_ref, xi_ref):
    nc = sr_ref.shape[0]
    ar, ai = ar_ref[...], ai_ref[...]

    def body(c, carry):
        xr, xi = carry
        xr_ref[c] = xr
        xi_ref[c] = xi
        return (ar * xr - ai * xi + sr_ref[c], ar * xi + ai * xr + si_ref[c])

    zero = jnp.zeros(ar.shape, F32)
    lax.fori_loop(0, nc, body, (zero, zero), unroll=8)


def _ssm_out_kernel(u_ref, km_ref, xr_ref, xi_ref, cr_ref, ci_ref, y_ref):
    xr = xr_ref[...].astype(BF16)
    xi = xi_ref[...].astype(BF16)
    for hh in range(2):
        y_ref[0, hh] = (jnp.dot(u_ref[0, hh], km_ref[hh], preferred_element_type=F32)
                        + jnp.dot(xr, cr_ref[hh], preferred_element_type=F32)
                        + jnp.dot(xi, ci_ref[hh], preferred_element_type=F32))


def ssm_mixer(u, a_re, a_im, log_dt, b_re, b_im, c_re, c_im):
    bsz, seq, sw = u.shape
    grp = sw // SSM_GROUP
    t_len = SSM_CHUNK
    nc = seq // t_len
    kw = t_len * SSM_GROUP
    kmat, bm_r, bm_i, cm_r, cm_i, at_r, at_i = _ssm_matrices(a_re, a_im, log_dt, b_re, b_im, c_re, c_im)
    p_st = at_r.shape[1]
    kmat = kmat.astype(BF16)
    bm_r, bm_i = _pair_pad(bm_r, 2).astype(BF16), _pair_pad(bm_i, 2).astype(BF16)
    cm_r, cm_i = _pair_pad(cm_r, 1).astype(BF16), _pair_pad(cm_i, 1).astype(BF16)
    ug = u.reshape(bsz, nc, t_len, grp, SSM_GROUP).transpose(0, 3, 1, 2, 4).reshape(bsz, grp, nc, kw)
    gp = grp // 2
    ncols = bsz * gp * 2 * p_st
    col = lambda b, g: (0, b * gp + g)
    u_spec = pl.BlockSpec((1, 2, nc, kw), lambda b, g: (b, g, 0, 0))
    s_spec = pl.BlockSpec((nc, 2 * p_st), col)
    s_re, s_im = pl.pallas_call(
        _ssm_state_kernel,
        grid=(bsz, gp),
        in_specs=[u_spec,
                  pl.BlockSpec((2, kw, 2 * p_st), lambda b, g: (g, 0, 0)),
                  pl.BlockSpec((2, kw, 2 * p_st), lambda b, g: (g, 0, 0))],
        out_specs=[s_spec, s_spec],
        out_shape=[jax.ShapeDtypeStruct((nc, ncols), F32)] * 2,
        compiler_params=_cparams(),
        name="ssm_state",
    )(ug, bm_r, bm_i)

    sub = SUBLANES
    lw = ncols // sub
    a_r = jnp.tile(at_r.reshape(-1), bsz).reshape(sub, lw)
    a_i = jnp.tile(at_i.reshape(-1), bsz).reshape(sub, lw)
    sc_spec = pl.BlockSpec((nc, sub, LANES), lambda j: (0, 0, j))
    a_spec = pl.BlockSpec((sub, LANES), lambda j: (0, j))
    x_re, x_im = pl.pallas_call(
        _ssm_scan_kernel,
        grid=(lw // LANES,),
        in_specs=[sc_spec, sc_spec, a_spec, a_spec],
        out_specs=[sc_spec, sc_spec],
        out_shape=[jax.ShapeDtypeStruct((nc, sub, lw), F32)] * 2,
        compiler_params=_cparams(),
        name="ssm_scan",
    )(s_re.reshape(nc, sub, lw), s_im.reshape(nc, sub, lw), a_r, a_i)

    y = pl.pallas_call(
        _ssm_out_kernel,
        grid=(bsz, gp),
        in_specs=[u_spec,
                  pl.BlockSpec((2, kw, kw), lambda b, g: (g, 0, 0)),
                  s_spec, s_spec,
                  pl.BlockSpec((2, 2 * p_st, kw), lambda b, g: (g, 0, 0)),
                  pl.BlockSpec((2, 2 * p_st, kw), lambda b, g: (g, 0, 0))],
        out_specs=pl.BlockSpec((1, 2, nc, kw), lambda b, g: (b, g, 0, 0)),
        out_shape=jax.ShapeDtypeStruct((bsz, grp, nc, kw), F32),
        compiler_params=_cparams(),
        name="ssm_out",
    )(ug, kmat, x_re.reshape(nc, ncols), x_im.reshape(nc, ncols), cm_r, cm_i)
    return y.reshape(bsz, grp, nc, t_len, SSM_GROUP).transpose(0, 2, 3, 1, 4).reshape(bsz, seq, sw)


def _merge_kernel(x_ref, y_ref, u_ref, dsk_ref, ot_ref, ga_ref, gb_ref, wglu_ref, wo_ref, wout_ref,
                  gt_ref, sh_ref, sc_ref, g_ref, wr_ref, br_ref,
                  x1_ref, h2_ref, ti_ref, tw_ref):
    d = x_ref.shape[2]
    n_exp = wr_ref.shape[0]
    y = y_ref[0] + dsk_ref[...] * u_ref[0].astype(F32)
    a = jax.nn.gelu(y, approximate=True).astype(BF16)
    glu = jnp.dot(a, wglu_ref[...], preferred_element_type=F32)
    y_a = glu[:, :d] * jax.nn.sigmoid(glu[:, d:])
    y_b = jnp.concatenate(
        [lax.dot_general(ot_ref[0, cb], wo_ref[...], (((0,), (0,)), ((), ())), preferred_element_type=F32)
         for cb in range(ot_ref.shape[1])], axis=0)
    merged = ga_ref[0].astype(F32) * y_a + gb_ref[0].astype(F32) * y_b
    z = jnp.dot(merged.astype(BF16), wout_ref[...], preferred_element_type=F32)
    x1 = x_ref[0] + gt_ref[0] * z
    x1_ref[0] = x1
    r = lax.rsqrt(jnp.mean(x1 * x1, axis=-1, keepdims=True) + RMS_EPS)
    h2 = (x1 * r * g_ref[...]) * (1.0 + sc_ref[0]) + sh_ref[0]
    h2_ref[0] = h2
    logits = lax.dot_general(wr_ref[...], h2.astype(BF16), (((1,), (1,)), ((), ())),
                             preferred_element_type=F32) + br_ref[...]
    eid = lax.broadcasted_iota(jnp.int32, (n_exp, 1), 0)
    vals, idxs = [], []
    g = logits
    for _ in range(TOP_K):
        m = jnp.max(g, axis=0, keepdims=True)
        first = jnp.min(jnp.where(g == m, eid, n_exp), axis=0, keepdims=True)
        vals.append(m)
        idxs.append(first)
        g = jnp.where(eid == first, -jnp.inf, g)
    v = jnp.concatenate(vals, axis=0)
    e = jnp.exp(v - v[0:1])
    tw_ref[0] = e / jnp.sum(e, axis=0, keepdims=True)
    ti_ref[0] = jnp.concatenate(idxs, axis=0)


def merge(x, y_ssm, u, d_skip, ot, ga, gb, w_glu, w_attn_o, w_out, gate, shift, scale, g, w_router, b_router,
          tm=256):
    bsz, seq, d = x.shape
    sw = u.shape[2]
    aw = ot.shape[2]
    n_exp = w_router.shape[1]
    tm = min(tm, seq)
    tok = lambda w: pl.BlockSpec((1, tm, w), lambda b, i: (b, i, 0))
    vec = pl.BlockSpec((1, 1, d), lambda b, i: (b, 0, 0))
    sel = pl.BlockSpec((1, TOP_K, tm), lambda b, i: (b, 0, i))
    return pl.pallas_call(
        _merge_kernel,
        grid=(bsz, seq // tm),
        in_specs=[tok(d), tok(sw), tok(sw), _const_spec((1, sw)),
                  pl.BlockSpec((1, tm // MOBA_BLOCK, aw, MOBA_BLOCK), lambda b, i: (b, i, 0, 0)),
                  tok(d), tok(d),
                  _const_spec(w_glu.shape), _const_spec(w_attn_o.shape), _const_spec(w_out.shape),
                  vec, vec, vec, _const_spec((1, d)), _const_spec((n_exp, d)), _const_spec((n_exp, 1))],
        out_specs=[tok(d), tok(d), sel, sel],
        out_shape=[jax.ShapeDtypeStruct((bsz, seq, d), F32),
                   jax.ShapeDtypeStruct((bsz, seq, d), F32),
                   jax.ShapeDtypeStruct((bsz, TOP_K, seq), jnp.int32),
                   jax.ShapeDtypeStruct((bsz, TOP_K, seq), F32)],
        compiler_params=_cparams(),
        name="merge",
    )(x, y_ssm, u, d_skip[None], ot, ga, gb, w_glu.astype(BF16), w_attn_o.astype(BF16), w_out.astype(BF16),
      gate[:, None], shift[:, None], scale[:, None], g[None], w_router.T.astype(BF16), b_router[:, None])


MOE_CHUNK = 256


def _moe_kernel(blk_e_ref, tok_hbm, dst_hbm, w_ref, h_hbm, wgu_ref, bgu_ref, wd_ref, bd_ref,
                out_hbm, tok_s, dst_s, xbuf, ybuf, xb_ref, act_ref, sem_idx, sem_g, sem_s):
    i = pl.program_id(0)
    nblk = pl.num_programs(0)
    rows = xbuf.shape[1]
    f = wd_ref.shape[1]
    slot = i % 2
    oslot = 1 - slot

    def tok_copy(b, s):
        return pltpu.make_async_copy(tok_hbm.at[b], tok_s.at[s], sem_idx.at[s])

    def dst_copy(b, s):
        return pltpu.make_async_copy(dst_hbm.at[b], dst_s.at[s], sem_idx.at[2 + s])

    def gather_row(r, s):
        return pltpu.make_async_copy(h_hbm.at[pl.ds(tok_s[s, 0, r], 1)], xbuf.at[s, pl.ds(r, 1)], sem_g.at[s])

    def scatter_row(r, s):
        return pltpu.make_async_copy(ybuf.at[s, pl.ds(r, 1)], out_hbm.at[pl.ds(dst_s[s, 0, r], 1)], sem_s.at[s])

    def gather_all(s):
        return pltpu.make_async_copy(h_hbm.at[pl.ds(0, rows)], xbuf.at[s], sem_g.at[s])

    def scatter_all(s):
        return pltpu.make_async_copy(ybuf.at[s], out_hbm.at[pl.ds(0, rows)], sem_s.at[s])

    @pl.when(i == 0)
    def _():
        tok_copy(0, 0).start()
        tok_copy(0, 0).wait()

        def first_rows(r, carry):
            gather_row(r, 0).start()
            return carry

        lax.fori_loop(0, rows, first_rows, 0)
        tok_copy(1, 1).start()
        dst_copy(nblk, 1).start()
        ybuf[1] = jnp.zeros(ybuf.shape[1:], F32)

    tok_copy(0, oslot).wait()
    dst_copy(0, oslot).wait()
    tok_copy(i + 2, slot).start()
    dst_copy(i, slot).start()
    gather_all(slot).wait()
    xb_ref[...] = xbuf[slot].astype(BF16)

    n_chunk = f // MOE_CHUNK
    per = rows // n_chunk
    for c in range(n_chunk):
        for r in range(c * per, (c + 1) * per):
            gather_row(r, oslot).start()
            scatter_row(r, oslot).start()
        lo, hi = c * MOE_CHUNK, (c + 1) * MOE_CHUNK
        xb = xb_ref[...]
        gate = jnp.dot(xb, wgu_ref[0, :, lo:hi], preferred_element_type=F32) + bgu_ref[0, :, lo:hi]
        up = jnp.dot(xb, wgu_ref[0, :, f + lo:f + hi], preferred_element_type=F32) + bgu_ref[0, :, f + lo:f + hi]
        gate = jnp.minimum(gate, SWIGLU_LIMIT)
        up = jnp.clip(up, -SWIGLU_LIMIT, SWIGLU_LIMIT)
        act_ref[:, lo:hi] = ((up + 1.0) * gate * jax.nn.sigmoid(SWIGLU_ALPHA * gate)).astype(BF16)
    y = jnp.dot(act_ref[...], wd_ref[0], preferred_element_type=F32) + bd_ref[0]
    ybuf[slot] = y * w_ref[0]
    scatter_all(oslot).wait()

    @pl.when(i == nblk - 1)
    def _():
        tok_copy(0, slot).wait()
        dst_copy(0, slot).wait()
        gather_all(oslot).wait()

        def last_rows(r, carry):
            scatter_row(r, slot).start()
            return carry

        lax.fori_loop(0, rows, last_rows, 0)
        scatter_all(slot).wait()


def moe_experts(h2, top_idx, top_w, w_gu, b_gu, w_down, b_down):
    n, d = h2.shape
    n_exp, _, f2 = w_gu.shape
    f = f2 // 2
    rb = EXPERT_ROWS
    nk = n * TOP_K
    i32 = jnp.int32
    flat_e = top_idx.reshape(nk)
    flat_w = top_w.reshape(nk)
    order = jnp.argsort(flat_e).astype(i32)
    counts = jnp.sum((flat_e[:, None] == jnp.arange(n_exp, dtype=i32)[None, :]).astype(i32), axis=0)
    group_start = jnp.cumsum(counts) - counts
    padded = (counts + rb - 1) // rb * rb
    padded_end = jnp.cumsum(padded)
    padded_start = padded_end - padded
    n_blocks = -(-nk // rb) + n_exp
    rows_total = n_blocks * rb
    blk_start = jnp.arange(n_blocks, dtype=i32) * rb
    blk_e = jnp.minimum(jnp.sum((padded_end[None, :] <= blk_start[:, None]).astype(i32), axis=1), n_exp - 1)
    row = jnp.arange(rows_total, dtype=i32)
    row_e = jnp.repeat(blk_e, rb)
    local = row - padded_start[row_e]
    valid = local < counts[row_e]
    pair = order[jnp.clip(group_start[row_e] + local, 0, nk - 1)]
    row_tok = jnp.where(valid, pair // TOP_K, 0)
    row_w = jnp.where(valid, flat_w[pair], 0.0)
    pad_rank = jnp.cumsum((~valid).astype(i32)) - 1
    row_dst = jnp.where(valid, pair, nk + pad_rank)
    tok_rows = jnp.concatenate([row_tok, jnp.zeros((2 * rb,), i32)]).reshape(n_blocks + 2, 1, rb)
    dst_rows = jnp.concatenate([row_dst, rows_total + jnp.arange(rb, dtype=i32)]).reshape(n_blocks + 1, 1, rb)

    grid_spec = pltpu.PrefetchScalarGridSpec(
        num_scalar_prefetch=1,
        grid=(n_blocks,),
        in_specs=[pl.BlockSpec(memory_space=pl.ANY),
                  pl.BlockSpec(memory_space=pl.ANY),
                  pl.BlockSpec((1, rb, 1), lambda i, be: (i, 0, 0)),
                  pl.BlockSpec(memory_space=pl.ANY),
                  pl.BlockSpec((1, d, f2), lambda i, be: (be[i], 0, 0)),
                  pl.BlockSpec((1, 1, f2), lambda i, be: (be[i], 0, 0)),
                  pl.BlockSpec((1, f, d), lambda i, be: (be[i], 0, 0)),
                  pl.BlockSpec((1, 1, d), lambda i, be: (be[i], 0, 0))],
        out_specs=pl.BlockSpec(memory_space=pl.ANY),
        scratch_shapes=[pltpu.SMEM((2, 1, rb), i32), pltpu.SMEM((2, 1, rb), i32),
                        pltpu.VMEM((2, rb, d), F32), pltpu.VMEM((2, rb, d), F32),
                        pltpu.VMEM((rb, d), BF16), pltpu.VMEM((rb, f), BF16),
                        pltpu.SemaphoreType.DMA((4,)), pltpu.SemaphoreType.DMA((2,)),
                        pltpu.SemaphoreType.DMA((2,))])
    return pl.pallas_call(
        _moe_kernel,
        grid_spec=grid_spec,
        out_shape=jax.ShapeDtypeStruct((rows_total + rb, d), F32),
        compiler_params=_cparams(),
        name="moe_experts",
    )(blk_e, tok_rows, dst_rows, row_w.reshape(n_blocks, rb, 1),
      h2, w_gu.astype(BF16), b_gu[:, None], w_down.astype(BF16), b_down[:, None])


def _combine_kernel(x_ref, y_ref, gt_ref, g_ref, o_ref):
    d = x_ref.shape[2]
    y = y_ref[...]
    moe = y[:, :d]
    for k in range(1, TOP_K):
        moe = moe + y[:, k * d:(k + 1) * d]
    x2 = x_ref[0] + gt_ref[0] * moe
    r = lax.rsqrt(jnp.mean(x2 * x2, axis=-1, keepdims=True) + RMS_EPS)
    o_ref[0] = x2 * r * g_ref[...]


def combine(x1, ytk, gate, g, tm=256):
    bsz, seq, d = x1.shape
    tm = min(tm, seq)
    nt = seq // tm
    return pl.pallas_call(
        _combine_kernel,
        grid=(bsz, nt),
        in_specs=[pl.BlockSpec((1, tm, d), lambda b, i: (b, i, 0)),
                  pl.BlockSpec((tm, TOP_K * d), lambda b, i: (b * nt + i, 0)),
                  pl.BlockSpec((1, 1, d), lambda b, i: (b, 0, 0)),
                  _const_spec((1, d))],
        out_specs=pl.BlockSpec((1, tm, d), lambda b, i: (b, i, 0)),
        out_shape=jax.ShapeDtypeStruct((bsz, seq, d), F32),
        compiler_params=_cparams(),
        name="combine",
    )(x1, ytk.reshape(ytk.shape[0] // TOP_K, TOP_K * d), gate[:, None], g[None])


def kernel(x, c, w_ada, b_ada, g_mix, w_in, ssm_a_re, ssm_a_im, ssm_log_dt, ssm_b_re, ssm_b_im, ssm_c_re,
           ssm_c_im, ssm_d, w_glu, w_attn_o, w_out, g_ffn, w_router, b_router, w_gu, b_gu, w_down, b_down,
           g_final):
    bsz, seq, d = x.shape
    depth = w_ada.shape[0]
    assert depth == 1
    sw = ssm_d.shape[1]
    aw = w_attn_o.shape[1]
    assert seq % MOBA_BLOCK == 0 and seq % SSM_CHUNK == 0
    for l in range(depth):
        mod = ada_mod(c, w_ada[l], b_ada[l])
        sh1, sc1, gt1, sh2, sc2, gt2 = jnp.split(mod, 6, axis=-1)
        u, k, kmean, ga, gb, qt, vt = in_proj(x, sh1, sc1, g_mix[l], w_in[l], sw, aw)
        ot = moba_attn(qt, k, vt, kmean.reshape(bsz, seq // MOBA_BLOCK, aw))
        y_ssm = ssm_mixer(u, ssm_a_re[l], ssm_a_im[l], ssm_log_dt[l], ssm_b_re[l], ssm_b_im[l],
                          ssm_c_re[l], ssm_c_im[l])
        x1, h2, top_idx, top_w = merge(x, y_ssm, u, ssm_d[l], ot, ga, gb, w_glu[l], w_attn_o[l], w_out[l],
                                       gt1, sh2, sc2, g_ffn[l], w_router[l], b_router[l])
        n = bsz * seq
        ytk = moe_experts(h2.reshape(n, d), top_idx.transpose(0, 2, 1).reshape(n, TOP_K),
                          top_w.transpose(0, 2, 1).reshape(n, TOP_K), w_gu[l], b_gu[l], w_down[l], b_down[l])
        x = combine(x1, ytk, gt2, g_final)
    return x
```

```python
import functools
import math

import jax
import jax.numpy as jnp
from jax import lax
from jax.experimental import pallas as pl
from jax.experimental.pallas import tpu as pltpu

F32 = jnp.float32
BF16 = jnp.bfloat16

SSM_GROUP = 16
SSM_CHUNK = 16
HEAD_DIM = 64
MOBA_BLOCK = 256
MOBA_TOPK = 3
ROPE_THETA = 500000.0
ROT_DIM = HEAD_DIM // 4
QK_SCALE = HEAD_DIM ** -0.5 * math.log2(math.e)
TOP_K = 4
SWIGLU_LIMIT = 7.0
SWIGLU_ALPHA = 1.702
EXPERT_ROWS = 256
RMS_EPS = 1e-5
LANES = 128
SUBLANES = 8
NEG = -1e30
VMEM_LIMIT = 56 * 1024 * 1024


def _cparams(**kw):
    return pltpu.CompilerParams(vmem_limit_bytes=VMEM_LIMIT, **kw)


def _const_spec(shape):
    nd = len(shape)
    return pl.BlockSpec(shape, lambda *_: (0,) * nd, pipeline_mode=pl.Buffered(1))


def _ada_kernel(c_ref, w_ref, b_ref, o_ref):
    c = c_ref[...]
    cond = c * jax.nn.sigmoid(c)
    o_ref[...] = jnp.dot(cond, w_ref[...], preferred_element_type=F32) + b_ref[...]


def ada_mod(c, w, b):
    bsz, d = c.shape
    n = w.shape[1]
    tn = min(n, 1024)
    return pl.pallas_call(
        _ada_kernel,
        grid=(n // tn,),
        in_specs=[pl.BlockSpec((bsz, d), lambda j: (0, 0)),
                  pl.BlockSpec((d, tn), lambda j: (0, j)),
                  pl.BlockSpec((1, tn), lambda j: (0, j))],
        out_specs=pl.BlockSpec((bsz, tn), lambda j: (0, j)),
        out_shape=jax.ShapeDtypeStruct((bsz, n), F32),
        compiler_params=_cparams(),
        name="ada_mod",
    )(c, w, b[None])


def _rope_tables(seq):
    half = ROT_DIM // 2
    inv_freq = 1.0 / (ROPE_THETA ** (jnp.arange(0, ROT_DIM, 2, dtype=F32) / ROT_DIM))
    ang = jnp.arange(seq, dtype=F32)[:, None] * inv_freq[None, :]
    cos, sin = jnp.cos(ang), jnp.sin(ang)
    d = jnp.arange(LANES) % HEAD_DIM
    f = d % half
    cos_l = jnp.where(d < ROT_DIM, cos[:, f], 1.0)
    sin_a = jnp.where(d < half, -sin[:, f], 0.0)
    sin_b = jnp.where((d >= half) & (d < ROT_DIM), sin[:, f], 0.0)
    return cos_l, sin_a, sin_b, cos.T, sin.T


def _inproj_kernel(x_ref, sh_ref, sc_ref, g_ref, wm_ref, wt_ref, cl_ref, sa_ref, sb_ref, ct_ref, st_ref,
                   u_ref, k_ref, km_ref, ga_ref, gb_ref, qt_ref, vt_ref, *, sw, aw):
    tm, d = x_ref.shape[1], x_ref.shape[2]
    half = ROT_DIM // 2
    xf = x_ref[0]
    r = lax.rsqrt(jnp.mean(xf * xf, axis=-1, keepdims=True) + RMS_EPS)
    h = (xf * r * g_ref[...]) * (1.0 + sc_ref[0]) + sh_ref[0]
    hb = h.astype(BF16)
    pm = jnp.dot(hb, wm_ref[...], preferred_element_type=F32)
    for q in range(sw // LANES):
        u_ref[0, q] = pm[:, q * LANES:(q + 1) * LANES]
    ga_ref[0] = jax.nn.sigmoid(pm[:, sw + aw:sw + aw + d]).astype(BF16)
    gb_ref[0] = jax.nn.sigmoid(pm[:, sw + aw + d:]).astype(BF16)

    cl, sa, sb = cl_ref[...], sa_ref[...], sb_ref[...]
    parts = []
    for t in range(aw // LANES):
        kt = pm[:, sw + t * LANES:sw + (t + 1) * LANES]
        parts.append(kt * cl + pltpu.roll(kt, LANES - half, 1) * sa + pltpu.roll(kt, half, 1) * sb)
    krot = jnp.concatenate(parts, axis=1)
    k_ref[0] = krot.astype(BF16)
    km_ref[0, 0] = jnp.mean(krot.reshape(tm // MOBA_BLOCK, MOBA_BLOCK, aw), axis=1)

    qv = lax.dot_general(wt_ref[...], hb, (((1,), (1,)), ((), ())), preferred_element_type=F32)
    ct, st = ct_ref[...], st_ref[...]
    rows = []
    for hd in range(aw // HEAD_DIM):
        base = hd * HEAD_DIM
        t1 = qv[base:base + half]
        t2 = qv[base + half:base + ROT_DIM]
        rows += [t1 * ct - t2 * st, t2 * ct + t1 * st, qv[base + ROT_DIM:base + HEAD_DIM]]
    qb = (jnp.concatenate(rows, axis=0) * QK_SCALE).astype(BF16)
    vb = qv[aw:].astype(BF16)
    for cb in range(tm // MOBA_BLOCK):
        qt_ref[0, cb] = qb[:, cb * MOBA_BLOCK:(cb + 1) * MOBA_BLOCK]
        vt_ref[0, cb] = vb[:, cb * MOBA_BLOCK:(cb + 1) * MOBA_BLOCK]


def in_proj(x, shift, scale, g, w_in, sw, aw, tm=512):
    bsz, seq, d = x.shape
    tm = min(tm, seq)
    nb = seq // MOBA_BLOCK
    wm = jnp.concatenate([w_in[:, :sw], w_in[:, sw + aw:sw + 2 * aw], w_in[:, sw + 3 * aw:]], axis=1).astype(BF16)
    wt = jnp.concatenate([w_in[:, sw:sw + aw], w_in[:, sw + 2 * aw:sw + 3 * aw]], axis=1).T.astype(BF16)
    cos_l, sin_a, sin_b, cos_t, sin_t = _rope_tables(seq)
    tok = lambda w: pl.BlockSpec((1, tm, w), lambda b, i: (b, i, 0))
    vec = pl.BlockSpec((1, 1, d), lambda b, i: (b, 0, 0))
    tab = pl.BlockSpec((tm, LANES), lambda b, i: (i, 0))
    tabt = pl.BlockSpec((ROT_DIM // 2, tm), lambda b, i: (0, i))
    blkt = pl.BlockSpec((1, tm // MOBA_BLOCK, aw, MOBA_BLOCK), lambda b, i: (b, i, 0, 0))
    return pl.pallas_call(
        functools.partial(_inproj_kernel, sw=sw, aw=aw),
        grid=(bsz, seq // tm),
        in_specs=[tok(d), vec, vec, _const_spec((1, d)), _const_spec(wm.shape), _const_spec(wt.shape),
                  tab, tab, tab, tabt, tabt],
        out_specs=[pl.BlockSpec((1, sw // LANES, tm, LANES), lambda b, i: (b, 0, i, 0)), tok(aw),
                   pl.BlockSpec((1, 1, tm // MOBA_BLOCK, aw), lambda b, i: (b, i, 0, 0)),
                   tok(d), tok(d), blkt, blkt],
        out_shape=[jax.ShapeDtypeStruct((bsz, sw // LANES, seq, LANES), F32),
                   jax.ShapeDtypeStruct((bsz, seq, aw), BF16),
                   jax.ShapeDtypeStruct((bsz, seq // tm, tm // MOBA_BLOCK, aw), F32),
                   jax.ShapeDtypeStruct((bsz, seq, d), BF16),
                   jax.ShapeDtypeStruct((bsz, seq, d), BF16),
                   jax.ShapeDtypeStruct((bsz, nb, aw, MOBA_BLOCK), BF16),
                   jax.ShapeDtypeStruct((bsz, nb, aw, MOBA_BLOCK), BF16)],
        compiler_params=_cparams(),
        name="in_proj",
    )(x, shift[:, None], scale[:, None], g[None], wm, wt, cos_l, sin_a, sin_b, cos_t, sin_t)


ATTN_GROUP = 4
SCORE_ROWS = 32


def _attn_kernel(q_ref, k_ref, v_ref, km_ref, o_ref, qs_ref, bias_ref, s0_ref, s1_ref, p0_ref, p1_ref, acc_ref):
    i = pl.program_id(2)
    nb = km_ref.shape[1]
    blk = MOBA_BLOCK
    gw = ATTN_GROUP * HEAD_DIM
    n_slab = blk // SCORE_ROWS
    q4 = q_ref[0, 0]
    rowid = lax.broadcasted_iota(jnp.int32, (gw, 1), 0)
    kmb = km_ref[0].astype(BF16)
    bid = lax.broadcasted_iota(jnp.int32, (nb, 1), 0)
    for h in range(ATTN_GROUP):
        in_head = (rowid >= h * HEAD_DIM) & (rowid < (h + 1) * HEAD_DIM)
        qh = jnp.where(in_head, q4, jnp.zeros_like(q4))
        qs_ref[h] = qh
        gate = jnp.dot(kmb, qh, preferred_element_type=F32)
        g = jnp.where(bid < i, gate, -jnp.inf)
        bias = jnp.full(g.shape, NEG, F32)
        for _ in range(MOBA_TOPK):
            m = jnp.max(g, axis=0, keepdims=True)
            first = jnp.min(jnp.where(g == m, bid, nb), axis=0, keepdims=True)
            hit = (bid == first) & (m > -jnp.inf)
            bias = jnp.where(hit, 0.0, bias)
            g = jnp.where(hit, -jnp.inf, g)
        bias_ref[h] = bias

    kpos = lax.broadcasted_iota(jnp.int32, (SCORE_ROWS, blk), 0)
    qpos = lax.broadcasted_iota(jnp.int32, (SCORE_ROWS, blk), 1)

    def scores(j, s_ref):
        for h in range(ATTN_GROUP):
            s_ref[h] = jnp.dot(k_ref[0, pl.ds(pl.multiple_of(j * blk, blk), blk), :], qs_ref[h],
                               preferred_element_type=F32)

    def weighted_values(j, p_ref):
        return [jnp.dot(v_ref[0, j, h * HEAD_DIM:(h + 1) * HEAD_DIM, :], p_ref[h], preferred_element_type=F32)
                for h in range(ATTN_GROUP)]

    def block_update(j, s_ref, p_ref, ms, ls, own, nxt=None, pend=None):
        if nxt is not None:
            scores(*nxt)
        if pend is not None:
            pvs = weighted_values(pend[0], pend[1])
        ms_new, ls_new, scales = [], [], []
        for h in range(ATTN_GROUP):
            def slab(c):
                s = s_ref[h, c * SCORE_ROWS:(c + 1) * SCORE_ROWS, :]
                if own:
                    s = jnp.where(kpos + c * SCORE_ROWS <= qpos, s, NEG)
                return s
            cm = slab(0)
            for c in range(1, n_slab):
                cm = jnp.maximum(cm, slab(c))
            cm = jnp.max(cm, axis=0, keepdims=True)
            if own:
                mn, shift = cm, cm
                scales.append(jnp.zeros_like(cm))
            else:
                row = bias_ref[h, pl.ds(j, 1), :]
                mn = jnp.maximum(ms[h], cm + row)
                shift = mn - row
                scales.append(jnp.exp2(ms[h] - mn))
            psum = jnp.zeros((SCORE_ROWS, blk), F32)
            for c in range(n_slab):
                p = jnp.exp2(slab(c) - shift)
                psum = psum + p
                p_ref[h, c * SCORE_ROWS:(c + 1) * SCORE_ROWS, :] = p.astype(BF16)
            lsum = jnp.sum(psum, axis=0, keepdims=True)
            ms_new.append(mn)
            ls_new.append(lsum if own else scales[h] * ls[h] + lsum)
        if pend is not None:
            for h in range(ATTN_GROUP):
                acc_ref[h] = pend[2][h] * acc_ref[h] + pvs[h]
        return ms_new, ls_new, scales

    acc_ref[...] = jnp.zeros(acc_ref.shape, F32)
    scores(i, s1_ref)
    scores(0, s0_ref)
    ms, ls, sc = block_update(i, s1_ref, p1_ref, None, None, True)
    g = ATTN_GROUP

    def body(t, carry):
        j = 2 * t
        prev = jnp.where(t == 0, i, j - 1)
        ms, ls, sc = block_update(j, s0_ref, p0_ref, carry[:g], carry[g:2 * g], False,
                                  nxt=(j + 1, s1_ref), pend=(prev, p1_ref, carry[2 * g:]))
        ms, ls, sc = block_update(j + 1, s1_ref, p1_ref, ms, ls, False,
                                  nxt=(jnp.minimum(j + 2, i), s0_ref), pend=(j, p0_ref, sc))
        return tuple(ms) + tuple(ls) + tuple(sc)

    n_pairs = (i + 1) // 2
    carry = lax.fori_loop(0, n_pairs, body, tuple(ms) + tuple(ls) + tuple(sc))
    last = jnp.where(n_pairs == 0, i, 2 * n_pairs - 1)
    pvs = weighted_values(last, p1_ref)
    for h in range(g):
        acc = carry[2 * g + h] * acc_ref[h] + pvs[h]
        o_ref[0, 0, h * HEAD_DIM:(h + 1) * HEAD_DIM, :] = (acc / carry[g + h]).astype(BF16)


def moba_attn(qt, k, vt, kmean):
    bsz, nb, aw, blk = qt.shape
    seq = k.shape[1]
    gw = ATTN_GROUP * HEAD_DIM
    return pl.pallas_call(
        _attn_kernel,
        grid=(bsz, aw // gw, nb),
        in_specs=[pl.BlockSpec((1, 1, gw, blk), lambda b, p, i: (b, i, p, 0)),
                  pl.BlockSpec((1, seq, gw), lambda b, p, i: (b, 0, p)),
                  pl.BlockSpec((1, nb, gw, blk), lambda b, p, i: (b, 0, p, 0)),
                  pl.BlockSpec((1, nb, gw), lambda b, p, i: (b, 0, p))],
        out_specs=pl.BlockSpec((1, 1, gw, blk), lambda b, p, i: (b, i, p, 0)),
        out_shape=jax.ShapeDtypeStruct((bsz, nb, aw, blk), BF16),
        scratch_shapes=[pltpu.VMEM((ATTN_GROUP, gw, blk), BF16),
                        pltpu.VMEM((ATTN_GROUP, nb, blk), F32),
                        pltpu.VMEM((ATTN_GROUP, blk, blk), F32),
                        pltpu.VMEM((ATTN_GROUP, blk, blk), F32),
                        pltpu.VMEM((ATTN_GROUP, blk, blk), BF16),
                        pltpu.VMEM((ATTN_GROUP, blk, blk), BF16),
                        pltpu.VMEM((ATTN_GROUP, HEAD_DIM, blk), F32)],
        compiler_params=_cparams(),
        name="moba_attn",
    )(qt, k, vt, kmean)


def _ssm_matrices(a_re, a_im, log_dt, b_re, b_im, c_re, c_im):
    grp, p_st = a_re.shape
    ch = b_re.shape[-1]
    gpt = LANES // ch
    nq = grp // gpt
    dt = jnp.exp(log_dt.astype(F32))[:, None]
    lr, li = a_re.astype(F32), a_im.astype(F32)

    def power(n):
        mag = jnp.exp(lr * dt * n)
        return mag * jnp.cos(li * dt * n), mag * jnp.sin(li * dt * n)

    abar_r, abar_i = power(1.0)
    at_r, at_i = power(float(SSM_CHUNK))
    den = lr * lr + li * li
    coef_r = ((abar_r - 1.0) * lr + abar_i * li) / den
    coef_i = (abar_i * lr - (abar_r - 1.0) * li) / den
    br, bi = b_re.astype(F32), b_im.astype(F32)
    bbar_r = coef_r[..., None] * br - coef_i[..., None] * bi
    bbar_i = coef_r[..., None] * bi + coef_i[..., None] * br
    eye = jnp.eye(gpt, dtype=F32)

    def diag_in(m):
        m = m.reshape(nq, gpt, p_st, ch)
        return jnp.einsum('qgpc,gh->qgchp', m, eye).reshape(nq, gpt * ch, gpt * p_st)

    def diag_out(m):
        m = m.reshape(nq, gpt, ch, p_st)
        return jnp.einsum('qgcp,gh->qgphc', m, eye).reshape(nq, gpt * p_st, gpt * ch)

    bd = jnp.concatenate([diag_in(bbar_r), diag_in(bbar_i)], axis=2)
    cd_r, cd_i = diag_out(c_re.astype(F32)), diag_out(-c_im.astype(F32))
    tile = lambda m: m.reshape(nq, 1, gpt * p_st)
    return tile(abar_r), tile(abar_i), at_r, at_i, bd, cd_r, cd_i


def _ssm_chunk_scan(u_ref, bd_ref, ar_ref, ai_ref, xr_ref, xi_ref, q, emit):
    n_rows, width = xr_ref.shape
    ar, ai = ar_ref[q], ai_ref[q]

    def body(j, carry):
        uj = u_ref[0, q, pl.ds(j, n_rows, stride=SSM_CHUNK), :].astype(BF16)
        bu = jnp.dot(uj, bd_ref[q], preferred_element_type=F32)
        xr, xi = xr_ref[...], xi_ref[...]
        nr = ar * xr - ai * xi + bu[:, :width]
        ni = ar * xi + ai * xr + bu[:, width:]
        xr_ref[...] = nr
        xi_ref[...] = ni
        if emit is not None:
            emit(j, nr, ni)
        return carry

    lax.fori_loop(0, SSM_CHUNK, body, 0)


def _ssm_state_kernel(u_ref, bd_ref, ar_ref, ai_ref, sr_ref, si_ref, xr_ref, xi_ref):
    width = xr_ref.shape[1]
    for q in range(bd_ref.shape[0]):
        xr_ref[...] = jnp.zeros(xr_ref.shape, F32)
        xi_ref[...] = jnp.zeros(xi_ref.shape, F32)
        _ssm_chunk_scan(u_ref, bd_ref, ar_ref, ai_ref, xr_ref, xi_ref, q, None)
        sr_ref[:, q * width:(q + 1) * width] = xr_ref[...]
        si_ref[:, q * width:(q + 1) * width] = xi_ref[...]


def _ssm_scan_kernel(sr_ref, si_ref, ar_ref, ai_ref, xr_ref, xi_ref):
    nc = sr_ref.shape[0]
    ar, ai = ar_ref[...], ai_ref[...]

    def body(c, carry):
        xr, xi = carry
        xr_ref[c] = xr
        xi_ref[c] = xi
        return (ar * xr - ai * xi + sr_ref[c], ar * xi + ai * xr + si_ref[c])

    zero = jnp.zeros(ar.shape, F32)
    lax.fori_loop(0, nc, body, (zero, zero), unroll=8)


def _ssm_out_kernel(u_ref, bd_ref, ar_ref, ai_ref, cr_ref, ci_ref, x0r_ref, x0i_ref, y_ref, xr_ref, xi_ref):
    n_rows, width = xr_ref.shape
    for q in range(bd_ref.shape[0]):
        xr_ref[...] = x0r_ref[:, q * width:(q + 1) * width]
        xi_ref[...] = x0i_ref[:, q * width:(q + 1) * width]

        def emit(j, xr, xi, q=q):
            y = (jnp.dot(xr.astype(BF16), cr_ref[q], preferred_element_type=F32)
                 + jnp.dot(xi.astype(BF16), ci_ref[q], preferred_element_type=F32))
            y_ref[0, q, pl.ds(j, n_rows, stride=SSM_CHUNK), :] = y

        _ssm_chunk_scan(u_ref, bd_ref, ar_ref, ai_ref, xr_ref, xi_ref, q, emit)


def ssm_mixer(u, a_re, a_im, log_dt, b_re, b_im, c_re, c_im, ts=4096):
    bsz, _, seq, _ = u.shape
    ts = min(ts, seq)
    nc = seq // SSM_CHUNK
    rows = ts // SSM_CHUNK
    abar_r, abar_i, at_r, at_i, bd, cd_r, cd_i = _ssm_matrices(a_re, a_im, log_dt, b_re, b_im, c_re, c_im)
    nq, _, width = abar_r.shape
    bd, cd_r, cd_i = bd.astype(BF16), cd_r.astype(BF16), cd_i.astype(BF16)
    ncols = bsz * nq * width
    u_spec = pl.BlockSpec((1, nq, ts, LANES), lambda b, i: (b, 0, i, 0))
    s_spec = pl.BlockSpec((rows, nq * width), lambda b, i: (i, b))
    consts = [_const_spec(bd.shape), _const_spec(abar_r.shape), _const_spec(abar_i.shape)]
    x_scratch = [pltpu.VMEM((rows, width), F32)] * 2
    s_re, s_im = pl.pallas_call(
        _ssm_state_kernel,
        grid=(bsz, seq // ts),
        in_specs=[u_spec] + consts,
        out_specs=[s_spec, s_spec],
        out_shape=[jax.ShapeDtypeStruct((nc, ncols), F32)] * 2,
        scratch_shapes=x_scratch,
        compiler_params=_cparams(),
        name="ssm_state",
    )(u, bd, abar_r, abar_i)

    sub = SUBLANES
    lw = ncols // sub
    a_r = jnp.tile(at_r.reshape(-1), bsz).reshape(sub, lw)
    a_i = jnp.tile(at_i.reshape(-1), bsz).reshape(sub, lw)
    sc_spec = pl.BlockSpec((nc, sub, LANES), lambda j: (0, 0, j))
    a_spec = pl.BlockSpec((sub, LANES), lambda j: (0, j))
    x_re, x_im = pl.pallas_call(
        _ssm_scan_kernel,
        grid=(lw // LANES,),
        in_specs=[sc_spec, sc_spec, a_spec, a_spec],
        out_specs=[sc_spec, sc_spec],
        out_shape=[jax.ShapeDtypeStruct((nc, sub, lw), F32)] * 2,
        compiler_params=_cparams(),
        name="ssm_scan",
    )(s_re.reshape(nc, sub, lw), s_im.reshape(nc, sub, lw), a_r, a_i)

    return pl.pallas_call(
        _ssm_out_kernel,
        grid=(bsz, seq // ts),
        in_specs=[u_spec] + consts + [_const_spec(cd_r.shape), _const_spec(cd_i.shape), s_spec, s_spec],
        out_specs=u_spec,
        out_shape=jax.ShapeDtypeStruct(u.shape, F32),
        scratch_shapes=x_scratch,
        compiler_params=_cparams(),
        name="ssm_out",
    )(u, bd, abar_r, abar_i, cd_r, cd_i, x_re.reshape(nc, ncols), x_im.reshape(nc, ncols))


def _merge_kernel(x_ref, y_ref, u_ref, dsk_ref, ot_ref, ga_ref, gb_ref, wglu_ref, wo_ref, wout_ref,
                  gt_ref, sh_ref, sc_ref, g_ref, wr_ref, br_ref,
                  x1_ref, h2_ref, ti_ref, tw_ref):
    d = x_ref.shape[2]
    n_exp = wr_ref.shape[0]
    nq = y_ref.shape[1]
    y = (jnp.concatenate([y_ref[0, q] for q in range(nq)], axis=1)
         + dsk_ref[...] * jnp.concatenate([u_ref[0, q] for q in range(nq)], axis=1))
    a = jax.nn.gelu(y, approximate=True).astype(BF16)
    glu = jnp.dot(a, wglu_ref[...], preferred_element_type=F32)
    y_a = glu[:, :d] * jax.nn.sigmoid(glu[:, d:])
    y_b = jnp.concatenate(
        [lax.dot_general(ot_ref[0, cb], wo_ref[...], (((0,), (0,)), ((), ())), preferred_element_type=F32)
         for cb in range(ot_ref.shape[1])], axis=0)
    merged = ga_ref[0].astype(F32) * y_a + gb_ref[0].astype(F32) * y_b
    z = jnp.dot(merged.astype(BF16), wout_ref[...], preferred_element_type=F32)
    x1 = x_ref[0] + gt_ref[0] * z
    x1_ref[0] = x1
    r = lax.rsqrt(jnp.mean(x1 * x1, axis=-1, keepdims=True) + RMS_EPS)
    h2 = (x1 * r * g_ref[...]) * (1.0 + sc_ref[0]) + sh_ref[0]
    h2_ref[0] = h2
    logits = lax.dot_general(wr_ref[...], h2.astype(BF16), (((1,), (1,)), ((), ())),
                             preferred_element_type=F32) + br_ref[...]
    eid = lax.broadcasted_iota(jnp.int32, (n_exp, 1), 0)
    vals, idxs = [], []
    g = logits
    for _ in range(TOP_K):
        m = jnp.max(g, axis=0, keepdims=True)
        first = jnp.min(jnp.where(g == m, eid, n_exp), axis=0, keepdims=True)
        vals.append(m)
        idxs.append(first)
        g = jnp.where(eid == first, -jnp.inf, g)
    v = jnp.concatenate(vals, axis=0)
    e = jnp.exp(v - v[0:1])
    tw_ref[0] = e / jnp.sum(e, axis=0, keepdims=True)
    ti_ref[0] = jnp.concatenate(idxs, axis=0)


def merge(x, y_ssm, u, d_skip, ot, ga, gb, w_glu, w_attn_o, w_out, gate, shift, scale, g, w_router, b_router,
          tm=256):
    bsz, seq, d = x.shape
    nq = u.shape[1]
    sw = nq * LANES
    aw = ot.shape[2]
    n_exp = w_router.shape[1]
    tm = min(tm, seq)
    tok = lambda w: pl.BlockSpec((1, tm, w), lambda b, i: (b, i, 0))
    tiles = pl.BlockSpec((1, nq, tm, LANES), lambda b, i: (b, 0, i, 0))
    vec = pl.BlockSpec((1, 1, d), lambda b, i: (b, 0, 0))
    sel = pl.BlockSpec((1, TOP_K, tm), lambda b, i: (b, 0, i))
    return pl.pallas_call(
        _merge_kernel,
        grid=(bsz, seq // tm),
        in_specs=[tok(d), tiles, tiles, _const_spec((1, sw)),
                  pl.BlockSpec((1, tm // MOBA_BLOCK, aw, MOBA_BLOCK), lambda b, i: (b, i, 0, 0)),
                  tok(d), tok(d),
                  _const_spec(w_glu.shape), _const_spec(w_attn_o.shape), _const_spec(w_out.shape),
                  vec, vec, vec, _const_spec((1, d)), _const_spec((n_exp, d)), _const_spec((n_exp, 1))],
        out_specs=[tok(d), tok(d), sel, sel],
        out_shape=[jax.ShapeDtypeStruct((bsz, seq, d), F32),
                   jax.ShapeDtypeStruct((bsz, seq, d), F32),
                   jax.ShapeDtypeStruct((bsz, TOP_K, seq), jnp.int32),
                   jax.ShapeDtypeStruct((bsz, TOP_K, seq), F32)],
        compiler_params=_cparams(),
        name="merge",
    )(x, y_ssm, u, d_skip[None], ot, ga, gb, w_glu.astype(BF16), w_attn_o.astype(BF16), w_out.astype(BF16),
      gate[:, None], shift[:, None], scale[:, None], g[None], w_router.T.astype(BF16), b_router[:, None])


MOE_CHUNK = 256


def _moe_kernel(blk_e_ref, tok_hbm, dst_hbm, w_ref, h_hbm, wgu_ref, bgu_ref, wd_ref, bd_ref,
                out_hbm, tok_s, dst_s, xbuf, ybuf, xb_ref, act_ref, sem_idx, sem_g, sem_s, *, n_blocks):
    i = pl.program_id(0)
    rows = xbuf.shape[1]
    f = wd_ref.shape[1]

    def tok_copy(b, s):
        return pltpu.make_async_copy(tok_hbm.at[b], tok_s.at[s], sem_idx.at[s])

    def dst_copy(b, s):
        return pltpu.make_async_copy(dst_hbm.at[b], dst_s.at[s], sem_idx.at[2 + s])

    def gather_row(r, s):
        return pltpu.make_async_copy(h_hbm.at[pl.ds(tok_s[s, 0, r], 1)], xbuf.at[s, pl.ds(r, 1)], sem_g.at[s])

    def scatter_row(r, s):
        return pltpu.make_async_copy(ybuf.at[s, pl.ds(r, 1)], out_hbm.at[pl.ds(dst_s[s, 0, r], 1)], sem_s.at[s])

    def gather_all(s):
        return pltpu.make_async_copy(h_hbm.at[pl.ds(0, rows)], xbuf.at[s], sem_g.at[s])

    def scatter_all(s):
        return pltpu.make_async_copy(ybuf.at[s], out_hbm.at[pl.ds(0, rows)], sem_s.at[s])

    @pl.when(i == 0)
    def _():
        tok_copy(0, 0).start()
        tok_copy(0, 0).wait()

        def first_rows(r, carry):
            gather_row(r, 0).start()
            return carry

        lax.fori_loop(0, rows, first_rows, 0)
        tok_copy(1, 1).start()
        dst_copy(n_blocks, 1).start()
        ybuf[1] = jnp.zeros(ybuf.shape[1:], F32)

    def step(slot):
        oslot = 1 - slot
        tok_copy(0, oslot).wait()
        dst_copy(0, oslot).wait()
        tok_copy(i + 2, slot).start()
        dst_copy(i, slot).start()
        for r in range(rows):
            scatter_row(r, oslot).start(priority=r % 2)
        gather_all(slot).wait()
        xb_ref[...] = xbuf[slot].astype(BF16)
        for r in range(rows):
            gather_row(r, oslot).start(priority=r % 2)

        for c in range(f // MOE_CHUNK):
            lo, hi = c * MOE_CHUNK, (c + 1) * MOE_CHUNK
            xb = xb_ref[...]
            gate = jnp.dot(xb, wgu_ref[0, :, lo:hi], preferred_element_type=F32) + bgu_ref[0, :, lo:hi]
            up = (jnp.dot(xb, wgu_ref[0, :, f + lo:f + hi], preferred_element_type=F32)
                  + bgu_ref[0, :, f + lo:f + hi])
            gate = jnp.minimum(gate, SWIGLU_LIMIT)
            up = jnp.clip(up, -SWIGLU_LIMIT, SWIGLU_LIMIT)
            act_ref[:, lo:hi] = ((up + 1.0) * gate * jax.nn.sigmoid(SWIGLU_ALPHA * gate)).astype(BF16)
        scatter_all(oslot).wait()
        ybuf[slot] = (jnp.dot(act_ref[...], wd_ref[0], preferred_element_type=F32) + bd_ref[0]) * w_ref[0]

    for s in range(2):
        pl.when(i % 2 == s)(functools.partial(step, s))

    @pl.when(i == n_blocks - 1)
    def _():
        slot = (n_blocks - 1) % 2
        tok_copy(0, slot).wait()
        dst_copy(0, slot).wait()
        gather_all(1 - slot).wait()

        def last_rows(r, carry):
            scatter_row(r, slot).start()
            return carry

        lax.fori_loop(0, rows, last_rows, 0)
        scatter_all(slot).wait()


def moe_experts(h2, top_idx, top_w, w_gu, b_gu, w_down, b_down):
    n, d = h2.shape
    n_exp, _, f2 = w_gu.shape
    f = f2 // 2
    rb = EXPERT_ROWS
    nk = n * TOP_K
    i32 = jnp.int32
    flat_e = top_idx.reshape(nk)
    flat_w = top_w.reshape(nk)
    order = jnp.argsort(flat_e).astype(i32)
    counts = jnp.sum((flat_e[:, None] == jnp.arange(n_exp, dtype=i32)[None, :]).astype(i32), axis=0)
    group_start = jnp.cumsum(counts) - counts
    padded = (counts + rb - 1) // rb * rb
    padded_end = jnp.cumsum(padded)
    padded_start = padded_end - padded
    n_blocks = -(-nk // rb) + n_exp
    rows_total = n_blocks * rb
    blk_start = jnp.arange(n_blocks, dtype=i32) * rb
    blk_e = jnp.minimum(jnp.sum((padded_end[None, :] <= blk_start[:, None]).astype(i32), axis=1), n_exp - 1)
    row = jnp.arange(rows_total, dtype=i32)
    row_e = jnp.repeat(blk_e, rb)
    local = row - padded_start[row_e]
    valid = local < counts[row_e]
    pair = order[jnp.clip(group_start[row_e] + local, 0, nk - 1)]
    row_tok = jnp.where(valid, pair // TOP_K, 0)
    row_w = jnp.where(valid, flat_w[pair], 0.0)
    pad_rank = jnp.cumsum((~valid).astype(i32)) - 1
    row_dst = jnp.where(valid, pair, nk + pad_rank)
    tok_rows = jnp.concatenate([row_tok, jnp.zeros((2 * rb,), i32)]).reshape(n_blocks + 2, 1, rb)
    dst_rows = jnp.concatenate([row_dst, rows_total + jnp.arange(rb, dtype=i32)]).reshape(n_blocks + 1, 1, rb)

    grid_spec = pltpu.PrefetchScalarGridSpec(
        num_scalar_prefetch=1,
        grid=(n_blocks,),
        in_specs=[pl.BlockSpec(memory_space=pl.ANY),
                  pl.BlockSpec(memory_space=pl.ANY),
                  pl.BlockSpec((1, rb, 1), lambda i, be: (i, 0, 0)),
                  pl.BlockSpec(memory_space=pl.ANY),
                  pl.BlockSpec((1, d, f2), lambda i, be: (be[i], 0, 0)),
                  pl.BlockSpec((1, 1, f2), lambda i, be: (be[i], 0, 0)),
                  pl.BlockSpec((1, f, d), lambda i, be: (be[i], 0, 0)),
                  pl.BlockSpec((1, 1, d), lambda i, be: (be[i], 0, 0))],
        out_specs=pl.BlockSpec(memory_space=pl.ANY),
        scratch_shapes=[pltpu.SMEM((2, 1, rb), i32), pltpu.SMEM((2, 1, rb), i32),
                        pltpu.VMEM((2, rb, d), F32), pltpu.VMEM((2, rb, d), F32),
                        pltpu.VMEM((rb, d), BF16), pltpu.VMEM((rb, f), BF16),
                        pltpu.SemaphoreType.DMA((4,)), pltpu.SemaphoreType.DMA((2,)),
                        pltpu.SemaphoreType.DMA((2,))])
    return pl.pallas_call(
        functools.partial(_moe_kernel, n_blocks=n_blocks),
        grid_spec=grid_spec,
        out_shape=jax.ShapeDtypeStruct((rows_total + rb, d), F32),
        compiler_params=_cparams(),
        name="moe_experts",
    )(blk_e, tok_rows, dst_rows, row_w.reshape(n_blocks, rb, 1),
      h2, w_gu.astype(BF16), b_gu[:, None], w_down.astype(BF16), b_down[:, None])


def _combine_kernel(x_ref, y_ref, gt_ref, g_ref, o_ref):
    d = x_ref.shape[2]
    moe = y_ref[:, 0:d]
    for k in range(1, TOP_K):
        moe = moe + y_ref[:, k * d:(k + 1) * d]
    x2 = x_ref[0] + gt_ref[0] * moe
    r = lax.rsqrt(jnp.mean(x2 * x2, axis=-1, keepdims=True) + RMS_EPS)
    o_ref[0] = x2 * r * g_ref[...]


def combine(x1, ytk, gate, g, tm=256):
    bsz, seq, d = x1.shape
    tm = min(tm, seq)
    nt = seq // tm
    return pl.pallas_call(
        _combine_kernel,
        grid=(bsz, nt),
        in_specs=[pl.BlockSpec((1, tm, d), lambda b, i: (b, i, 0)),
                  pl.BlockSpec((tm, TOP_K * d), lambda b, i: (b * nt + i, 0)),
                  pl.BlockSpec((1, 1, d), lambda b, i: (b, 0, 0)),
                  _const_spec((1, d))],
        out_specs=pl.BlockSpec((1, tm, d), lambda b, i: (b, i, 0)),
        out_shape=jax.ShapeDtypeStruct((bsz, seq, d), F32),
        compiler_params=_cparams(),
        name="combine",
    )(x1, ytk.reshape(ytk.shape[0] // TOP_K, TOP_K * d), gate[:, None], g[None])


def kernel(x, c, w_ada, b_ada, g_mix, w_in, ssm_a_re, ssm_a_im, ssm_log_dt, ssm_b_re, ssm_b_im, ssm_c_re,
           ssm_c_im, ssm_d, w_glu, w_attn_o, w_out, g_ffn, w_router, b_router, w_gu, b_gu, w_down, b_down,
           g_final):
    bsz, seq, d = x.shape
    depth = w_ada.shape[0]
    assert depth == 1
    sw = ssm_d.shape[1]
    aw = w_attn_o.shape[1]
    assert seq % MOBA_BLOCK == 0 and seq % SSM_CHUNK == 0
    for l in range(depth):
        mod = ada_mod(c, w_ada[l], b_ada[l])
        sh1, sc1, gt1, sh2, sc2, gt2 = jnp.split(mod, 6, axis=-1)
        u, k, kmean, ga, gb, qt, vt = in_proj(x, sh1, sc1, g_mix[l], w_in[l], sw, aw)
        ot = moba_attn(qt, k, vt, kmean.reshape(bsz, seq // MOBA_BLOCK, aw))
        y_ssm = ssm_mixer(u, ssm_a_re[l], ssm_a_im[l], ssm_log_dt[l], ssm_b_re[l], ssm_b_im[l],
                          ssm_c_re[l], ssm_c_im[l])
        x1, h2, top_idx, top_w = merge(x, y_ssm, u, ssm_d[l], ot, ga, gb, w_glu[l], w_attn_o[l], w_out[l],
                                       gt1, sh2, sc2, g_ffn[l], w_router[l], b_router[l])
        n = bsz * seq
        ytk = moe_experts(h2.reshape(n, d), top_idx.transpose(0, 2, 1).reshape(n, TOP_K),
                          top_w.transpose(0, 2, 1).reshape(n, TOP_K), w_gu[l], b_gu[l], w_down[l], b_down[l])
        x = combine(x1, ytk, gt2, g_final)
    return x
```

```python
import functools
import math

import jax
import jax.numpy as jnp
from jax import lax
from jax.experimental import pallas as pl
from jax.experimental.pallas import tpu as pltpu

F32 = jnp.float32
BF16 = jnp.bfloat16

SSM_GROUP = 16
SSM_CHUNK = 16
HEAD_DIM = 64
MOBA_BLOCK = 256
MOBA_TOPK = 3
ROPE_THETA = 500000.0
ROT_DIM = HEAD_DIM // 4
QK_SCALE = HEAD_DIM ** -0.5 * math.log2(math.e)
TOP_K = 4
SWIGLU_LIMIT = 7.0
SWIGLU_ALPHA = 1.702
EXPERT_ROWS = 256
RMS_EPS = 1e-5
LANES = 128
SUBLANES = 8
NEG = -1e30
VMEM_LIMIT = 56 * 1024 * 1024


def _cparams(**kw):
    return pltpu.CompilerParams(vmem_limit_bytes=VMEM_LIMIT, **kw)


def _const_spec(shape):
    nd = len(shape)
    return pl.BlockSpec(shape, lambda *_: (0,) * nd, pipeline_mode=pl.Buffered(1))


def _ada_kernel(c_ref, w_ref, b_ref, o_ref):
    c = c_ref[...]
    cond = c * jax.nn.sigmoid(c)
    o_ref[...] = jnp.dot(cond, w_ref[...], preferred_element_type=F32) + b_ref[...]


def ada_mod(c, w, b):
    bsz, d = c.shape
    n = w.shape[1]
    tn = min(n, 1024)
    return pl.pallas_call(
        _ada_kernel,
        grid=(n // tn,),
        in_specs=[pl.BlockSpec((bsz, d), lambda j: (0, 0)),
                  pl.BlockSpec((d, tn), lambda j: (0, j)),
                  pl.BlockSpec((1, tn), lambda j: (0, j))],
        out_specs=pl.BlockSpec((bsz, tn), lambda j: (0, j)),
        out_shape=jax.ShapeDtypeStruct((bsz, n), F32),
        compiler_params=_cparams(),
        name="ada_mod",
    )(c, w, b[None])


def _rope_tables(seq):
    half = ROT_DIM // 2
    inv_freq = 1.0 / (ROPE_THETA ** (jnp.arange(0, ROT_DIM, 2, dtype=F32) / ROT_DIM))
    ang = jnp.arange(seq, dtype=F32)[:, None] * inv_freq[None, :]
    cos, sin = jnp.cos(ang), jnp.sin(ang)
    d = jnp.arange(LANES) % HEAD_DIM
    f = d % half
    cos_l = jnp.where(d < ROT_DIM, cos[:, f], 1.0)
    sin_a = jnp.where(d < half, -sin[:, f], 0.0)
    sin_b = jnp.where((d >= half) & (d < ROT_DIM), sin[:, f], 0.0)
    return cos_l, sin_a, sin_b, cos.T, sin.T


def _inproj_kernel(x_ref, sh_ref, sc_ref, g_ref, wm_ref, wt_ref, cl_ref, sa_ref, sb_ref, ct_ref, st_ref,
                   u_ref, k_ref, km_ref, ga_ref, gb_ref, qt_ref, vt_ref, *, sw, aw):
    tm, d = x_ref.shape[1], x_ref.shape[2]
    half = ROT_DIM // 2
    xf = x_ref[0]
    r = lax.rsqrt(jnp.mean(xf * xf, axis=-1, keepdims=True) + RMS_EPS)
    h = (xf * r * g_ref[...]) * (1.0 + sc_ref[0]) + sh_ref[0]
    hb = h.astype(BF16)
    pm = jnp.dot(hb, wm_ref[...], preferred_element_type=F32)
    for q in range(sw // LANES):
        u_ref[0, q] = pm[:, q * LANES:(q + 1) * LANES]
    ga_ref[0] = jax.nn.sigmoid(pm[:, sw + aw:sw + aw + d]).astype(BF16)
    gb_ref[0] = jax.nn.sigmoid(pm[:, sw + aw + d:]).astype(BF16)

    cl, sa, sb = cl_ref[...], sa_ref[...], sb_ref[...]
    parts = []
    for t in range(aw // LANES):
        kt = pm[:, sw + t * LANES:sw + (t + 1) * LANES]
        parts.append(kt * cl + pltpu.roll(kt, LANES - half, 1) * sa + pltpu.roll(kt, half, 1) * sb)
    krot = jnp.concatenate(parts, axis=1)
    k_ref[0] = krot.astype(BF16)
    km_ref[0, 0] = jnp.mean(krot.reshape(tm // MOBA_BLOCK, MOBA_BLOCK, aw), axis=1)

    qv = lax.dot_general(wt_ref[...], hb, (((1,), (1,)), ((), ())), preferred_element_type=F32)
    ct, st = ct_ref[...], st_ref[...]
    rows = []
    for hd in range(aw // HEAD_DIM):
        base = hd * HEAD_DIM
        t1 = qv[base:base + half]
        t2 = qv[base + half:base + ROT_DIM]
        rows += [t1 * ct - t2 * st, t2 * ct + t1 * st, qv[base + ROT_DIM:base + HEAD_DIM]]
    qb = (jnp.concatenate(rows, axis=0) * QK_SCALE).astype(BF16)
    vb = qv[aw:].astype(BF16)
    for cb in range(tm // MOBA_BLOCK):
        qt_ref[0, cb] = qb[:, cb * MOBA_BLOCK:(cb + 1) * MOBA_BLOCK]
        vt_ref[0, cb] = vb[:, cb * MOBA_BLOCK:(cb + 1) * MOBA_BLOCK]


def in_proj(x, shift, scale, g, w_in, sw, aw, tm=512):
    bsz, seq, d = x.shape
    tm = min(tm, seq)
    nb = seq // MOBA_BLOCK
    wm = jnp.concatenate([w_in[:, :sw], w_in[:, sw + aw:sw + 2 * aw], w_in[:, sw + 3 * aw:]], axis=1).astype(BF16)
    wt = jnp.concatenate([w_in[:, sw:sw + aw], w_in[:, sw + 2 * aw:sw + 3 * aw]], axis=1).T.astype(BF16)
    cos_l, sin_a, sin_b, cos_t, sin_t = _rope_tables(seq)
    tok = lambda w: pl.BlockSpec((1, tm, w), lambda b, i: (b, i, 0))
    vec = pl.BlockSpec((1, 1, d), lambda b, i: (b, 0, 0))
    tab = pl.BlockSpec((tm, LANES), lambda b, i: (i, 0))
    tabt = pl.BlockSpec((ROT_DIM // 2, tm), lambda b, i: (0, i))
    blkt = pl.BlockSpec((1, tm // MOBA_BLOCK, aw, MOBA_BLOCK), lambda b, i: (b, i, 0, 0))
    return pl.pallas_call(
        functools.partial(_inproj_kernel, sw=sw, aw=aw),
        grid=(bsz, seq // tm),
        in_specs=[tok(d), vec, vec, _const_spec((1, d)), _const_spec(wm.shape), _const_spec(wt.shape),
                  tab, tab, tab, tabt, tabt],
        out_specs=[pl.BlockSpec((1, sw // LANES, tm, LANES), lambda b, i: (b, 0, i, 0)), tok(aw),
                   pl.BlockSpec((1, 1, tm // MOBA_BLOCK, aw), lambda b, i: (b, i, 0, 0)),
                   tok(d), tok(d), blkt, blkt],
        out_shape=[jax.ShapeDtypeStruct((bsz, sw // LANES, seq, LANES), F32),
                   jax.ShapeDtypeStruct((bsz, seq, aw), BF16),
                   jax.ShapeDtypeStruct((bsz, seq // tm, tm // MOBA_BLOCK, aw), F32),
                   jax.ShapeDtypeStruct((bsz, seq, d), BF16),
                   jax.ShapeDtypeStruct((bsz, seq, d), BF16),
                   jax.ShapeDtypeStruct((bsz, nb, aw, MOBA_BLOCK), BF16),
                   jax.ShapeDtypeStruct((bsz, nb, aw, MOBA_BLOCK), BF16)],
        compiler_params=_cparams(),
        name="in_proj",
    )(x, shift[:, None], scale[:, None], g[None], wm, wt, cos_l, sin_a, sin_b, cos_t, sin_t)


ATTN_GROUP = 4
SCORE_ROWS = 32
ATTN_UNROLL = 2


def _attn_kernel(q_ref, k_ref, v_ref, km_ref, o_ref, qs_ref, bias_ref, s0_ref, s1_ref, p0_ref, p1_ref, acc_ref):
    i = pl.program_id(2)
    nb = km_ref.shape[1]
    blk = MOBA_BLOCK
    gw = ATTN_GROUP * HEAD_DIM
    n_slab = blk // SCORE_ROWS
    q4 = q_ref[0, 0]
    rowid = lax.broadcasted_iota(jnp.int32, (gw, 1), 0)
    kmb = km_ref[0].astype(BF16)
    bid = lax.broadcasted_iota(jnp.int32, (nb, 1), 0)
    for h in range(ATTN_GROUP):
        in_head = (rowid >= h * HEAD_DIM) & (rowid < (h + 1) * HEAD_DIM)
        qh = jnp.where(in_head, q4, jnp.zeros_like(q4))
        qs_ref[h] = qh
        gate = jnp.dot(kmb, qh, preferred_element_type=F32)
        g = jnp.where(bid < i, gate, -jnp.inf)
        bias = jnp.full(g.shape, NEG, F32)
        for _ in range(MOBA_TOPK):
            m = jnp.max(g, axis=0, keepdims=True)
            first = jnp.min(jnp.where(g == m, bid, nb), axis=0, keepdims=True)
            hit = (bid == first) & (m > -jnp.inf)
            bias = jnp.where(hit, 0.0, bias)
            g = jnp.where(hit, -jnp.inf, g)
        bias_ref[h] = bias

    kpos = lax.broadcasted_iota(jnp.int32, (SCORE_ROWS, blk), 0)
    qpos = lax.broadcasted_iota(jnp.int32, (SCORE_ROWS, blk), 1)

    def scores(j, s_ref):
        for h in range(ATTN_GROUP):
            s_ref[h] = jnp.dot(k_ref[0, pl.ds(pl.multiple_of(j * blk, blk), blk), :], qs_ref[h],
                               preferred_element_type=F32)

    def weighted_values(j, p_ref):
        return [jnp.dot(v_ref[0, j, h * HEAD_DIM:(h + 1) * HEAD_DIM, :], p_ref[h], preferred_element_type=F32)
                for h in range(ATTN_GROUP)]

    def block_update(j, s_ref, p_ref, ms, ls, own, nxt=None, pend=None):
        if nxt is not None:
            scores(*nxt)
        if pend is not None:
            pvs = weighted_values(pend[0], pend[1])
        ms_new, ls_new, scales = [], [], []
        for h in range(ATTN_GROUP):
            def slab(c):
                s = s_ref[h, c * SCORE_ROWS:(c + 1) * SCORE_ROWS, :]
                if own:
                    s = jnp.where(kpos + c * SCORE_ROWS <= qpos, s, NEG)
                return s
            cm = slab(0)
            for c in range(1, n_slab):
                cm = jnp.maximum(cm, slab(c))
            cm = jnp.max(cm, axis=0, keepdims=True)
            if own:
                mn, shift = cm, cm
                scales.append(jnp.zeros_like(cm))
            else:
                row = bias_ref[h, pl.ds(j, 1), :]
                mn = jnp.maximum(ms[h], cm + row)
                shift = mn - row
                scales.append(jnp.exp2(ms[h] - mn))
            psum = jnp.zeros((SCORE_ROWS, blk), F32)
            for c in range(n_slab):
                p = jnp.exp2(slab(c) - shift)
                psum = psum + p
                p_ref[h, c * SCORE_ROWS:(c + 1) * SCORE_ROWS, :] = p.astype(BF16)
            lsum = jnp.sum(psum, axis=0, keepdims=True)
            ms_new.append(mn)
            ls_new.append(lsum if own else scales[h] * ls[h] + lsum)
        if pend is not None:
            for h in range(ATTN_GROUP):
                acc_ref[h] = pend[2][h] * acc_ref[h] + pvs[h]
        return ms_new, ls_new, scales

    acc_ref[...] = jnp.zeros(acc_ref.shape, F32)
    scores(i, s1_ref)
    scores(0, s0_ref)
    ms, ls, sc = block_update(i, s1_ref, p1_ref, None, None, True)
    g = ATTN_GROUP

    bufs = ((s0_ref, p0_ref), (s1_ref, p1_ref))

    def body(t, carry):
        ms, ls, sc = carry[:g], carry[g:2 * g], carry[2 * g:]
        for un in range(ATTN_UNROLL):
            j = ATTN_UNROLL * t + un
            prev = jnp.where(j == 0, i, jnp.minimum(j - 1, i))
            s_cur, p_cur = bufs[un % 2]
            s_oth, p_oth = bufs[1 - un % 2]
            ms, ls, sc = block_update(jnp.minimum(j, i), s_cur, p_cur, ms, ls, False,
                                      nxt=(jnp.minimum(j + 1, i), s_oth), pend=(prev, p_oth, sc))
        return tuple(ms) + tuple(ls) + tuple(sc)

    n_trips = (i + ATTN_UNROLL - 1) // ATTN_UNROLL
    carry = lax.fori_loop(0, n_trips, body, tuple(ms) + tuple(ls) + tuple(sc))
    last = jnp.where(n_trips == 0, i, jnp.minimum(ATTN_UNROLL * n_trips - 1, i))
    pvs = weighted_values(last, p1_ref)
    for h in range(g):
        acc = carry[2 * g + h] * acc_ref[h] + pvs[h]
        o_ref[0, 0, h * HEAD_DIM:(h + 1) * HEAD_DIM, :] = (acc / carry[g + h]).astype(BF16)


def moba_attn(qt, k, vt, kmean):
    bsz, nb, aw, blk = qt.shape
    seq = k.shape[1]
    gw = ATTN_GROUP * HEAD_DIM
    return pl.pallas_call(
        _attn_kernel,
        grid=(bsz, aw // gw, nb),
        in_specs=[pl.BlockSpec((1, 1, gw, blk), lambda b, p, i: (b, i, p, 0)),
                  pl.BlockSpec((1, seq, gw), lambda b, p, i: (b, 0, p)),
                  pl.BlockSpec((1, nb, gw, blk), lambda b, p, i: (b, 0, p, 0)),
                  pl.BlockSpec((1, nb, gw), lambda b, p, i: (b, 0, p))],
        out_specs=pl.BlockSpec((1, 1, gw, blk), lambda b, p, i: (b, i, p, 0)),
        out_shape=jax.ShapeDtypeStruct((bsz, nb, aw, blk), BF16),
        scratch_shapes=[pltpu.VMEM((ATTN_GROUP, gw, blk), BF16),
                        pltpu.VMEM((ATTN_GROUP, nb, blk), F32),
                        pltpu.VMEM((ATTN_GROUP, blk, blk), F32),
                        pltpu.VMEM((ATTN_GROUP, blk, blk), F32),
                        pltpu.VMEM((ATTN_GROUP, blk, blk), BF16),
                        pltpu.VMEM((ATTN_GROUP, blk, blk), BF16),
                        pltpu.VMEM((ATTN_GROUP, HEAD_DIM, blk), F32)],
        compiler_params=_cparams(),
        name="moba_attn",
    )(qt, k, vt, kmean)


def _ssm_matrices(a_re, a_im, log_dt, b_re, b_im, c_re, c_im):
    grp, p_st = a_re.shape
    ch = b_re.shape[-1]
    gpt = LANES // ch
    nq = grp // gpt
    dt = jnp.exp(log_dt.astype(F32))[:, None]
    lr, li = a_re.astype(F32), a_im.astype(F32)

    def power(n):
        mag = jnp.exp(lr * dt * n)
        return mag * jnp.cos(li * dt * n), mag * jnp.sin(li * dt * n)

    abar_r, abar_i = power(1.0)
    at_r, at_i = power(float(SSM_CHUNK))
    den = lr * lr + li * li
    coef_r = ((abar_r - 1.0) * lr + abar_i * li) / den
    coef_i = (abar_i * lr - (abar_r - 1.0) * li) / den
    br, bi = b_re.astype(F32), b_im.astype(F32)
    bbar_r = coef_r[..., None] * br - coef_i[..., None] * bi
    bbar_i = coef_r[..., None] * bi + coef_i[..., None] * br
    eye = jnp.eye(gpt, dtype=F32)

    def diag_in(m):
        m = m.reshape(nq, gpt, p_st, ch)
        return jnp.einsum('qgpc,gh->qgchp', m, eye).reshape(nq, gpt * ch, gpt * p_st)

    def diag_out(m):
        m = m.reshape(nq, gpt, ch, p_st)
        return jnp.einsum('qgcp,gh->qgphc', m, eye).reshape(nq, gpt * p_st, gpt * ch)

    bd = jnp.concatenate([diag_in(bbar_r), diag_in(bbar_i)], axis=2)
    cd_r, cd_i = diag_out(c_re.astype(F32)), diag_out(-c_im.astype(F32))
    tile = lambda m: m.reshape(nq, 1, gpt * p_st)
    return tile(abar_r), tile(abar_i), at_r, at_i, bd, cd_r, cd_i


def _ssm_chunk_scan(u_ref, bd_ref, ar_ref, ai_ref, xr_ref, xi_ref, q, emit):
    n_rows, width = xr_ref.shape
    ar, ai = ar_ref[q], ai_ref[q]

    def body(j, carry):
        uj = u_ref[0, q, pl.ds(j, n_rows, stride=SSM_CHUNK), :].astype(BF16)
        bu = jnp.dot(uj, bd_ref[q], preferred_element_type=F32)
        xr, xi = xr_ref[...], xi_ref[...]
        nr = ar * xr - ai * xi + bu[:, :width]
        ni = ar * xi + ai * xr + bu[:, width:]
        xr_ref[...] = nr
        xi_ref[...] = ni
        if emit is not None:
            emit(j, nr, ni)
        return carry

    lax.fori_loop(0, SSM_CHUNK, body, 0)


def _ssm_state_kernel(u_ref, bd_ref, ar_ref, ai_ref, sr_ref, si_ref, xr_ref, xi_ref):
    width = xr_ref.shape[1]
    for q in range(bd_ref.shape[0]):
        xr_ref[...] = jnp.zeros(xr_ref.shape, F32)
        xi_ref[...] = jnp.zeros(xi_ref.shape, F32)
        _ssm_chunk_scan(u_ref, bd_ref, ar_ref, ai_ref, xr_ref, xi_ref, q, None)
        sr_ref[:, q * width:(q + 1) * width] = xr_ref[...]
        si_ref[:, q * width:(q + 1) * width] = xi_ref[...]


def _ssm_scan_kernel(sr_ref, si_ref, ar_ref, ai_ref, xr_ref, xi_ref):
    nc = sr_ref.shape[0]
    ar, ai = ar_ref[...], ai_ref[...]

    def body(c, carry):
        xr, xi = carry
        xr_ref[c] = xr
        xi_ref[c] = xi
        return (ar * xr - ai * xi + sr_ref[c], ar * xi + ai * xr + si_ref[c])

    zero = jnp.zeros(ar.shape, F32)
    lax.fori_loop(0, nc, body, (zero, zero), unroll=8)


def _ssm_out_kernel(u_ref, bd_ref, ar_ref, ai_ref, cr_ref, ci_ref, x0r_ref, x0i_ref, y_ref, xr_ref, xi_ref):
    n_rows, width = xr_ref.shape
    for q in range(bd_ref.shape[0]):
        xr_ref[...] = x0r_ref[:, q * width:(q + 1) * width]
        xi_ref[...] = x0i_ref[:, q * width:(q + 1) * width]

        def emit(j, xr, xi, q=q):
            y = (jnp.dot(xr.astype(BF16), cr_ref[q], preferred_element_type=F32)
                 + jnp.dot(xi.astype(BF16), ci_ref[q], preferred_element_type=F32))
            y_ref[0, q, pl.ds(j, n_rows, stride=SSM_CHUNK), :] = y

        _ssm_chunk_scan(u_ref, bd_ref, ar_ref, ai_ref, xr_ref, xi_ref, q, emit)


def ssm_mixer(u, a_re, a_im, log_dt, b_re, b_im, c_re, c_im, ts=4096):
    bsz, _, seq, _ = u.shape
    ts = min(ts, seq)
    nc = seq // SSM_CHUNK
    rows = ts // SSM_CHUNK
    abar_r, abar_i, at_r, at_i, bd, cd_r, cd_i = _ssm_matrices(a_re, a_im, log_dt, b_re, b_im, c_re, c_im)
    nq, _, width = abar_r.shape
    bd, cd_r, cd_i = bd.astype(BF16), cd_r.astype(BF16), cd_i.astype(BF16)
    ncols = bsz * nq * width
    u_spec = pl.BlockSpec((1, nq, ts, LANES), lambda b, i: (b, 0, i, 0))
    s_spec = pl.BlockSpec((rows, nq * width), lambda b, i: (i, b))
    consts = [_const_spec(bd.shape), _const_spec(abar_r.shape), _const_spec(abar_i.shape)]
    x_scratch = [pltpu.VMEM((rows, width), F32)] * 2
    s_re, s_im = pl.pallas_call(
        _ssm_state_kernel,
        grid=(bsz, seq // ts),
        in_specs=[u_spec] + consts,
        out_specs=[s_spec, s_spec],
        out_shape=[jax.ShapeDtypeStruct((nc, ncols), F32)] * 2,
        scratch_shapes=x_scratch,
        compiler_params=_cparams(),
        name="ssm_state",
    )(u, bd, abar_r, abar_i)

    sub = SUBLANES
    lw = ncols // sub
    a_r = jnp.tile(at_r.reshape(-1), bsz).reshape(sub, lw)
    a_i = jnp.tile(at_i.reshape(-1), bsz).reshape(sub, lw)
    sc_spec = pl.BlockSpec((nc, sub, LANES), lambda j: (0, 0, j))
    a_spec = pl.BlockSpec((sub, LANES), lambda j: (0, j))
    x_re, x_im = pl.pallas_call(
        _ssm_scan_kernel,
        grid=(lw // LANES,),
        in_specs=[sc_spec, sc_spec, a_spec, a_spec],
        out_specs=[sc_spec, sc_spec],
        out_shape=[jax.ShapeDtypeStruct((nc, sub, lw), F32)] * 2,
        compiler_params=_cparams(),
        name="ssm_scan",
    )(s_re.reshape(nc, sub, lw), s_im.reshape(nc, sub, lw), a_r, a_i)

    return pl.pallas_call(
        _ssm_out_kernel,
        grid=(bsz, seq // ts),
        in_specs=[u_spec] + consts + [_const_spec(cd_r.shape), _const_spec(cd_i.shape), s_spec, s_spec],
        out_specs=u_spec,
        out_shape=jax.ShapeDtypeStruct(u.shape, F32),
        scratch_shapes=x_scratch,
        compiler_params=_cparams(),
        name="ssm_out",
    )(u, bd, abar_r, abar_i, cd_r, cd_i, x_re.reshape(nc, ncols), x_im.reshape(nc, ncols))


def _merge_kernel(x_ref, y_ref, u_ref, dsk_ref, ot_ref, ga_ref, gb_ref, wglu_ref, wo_ref, wout_ref,
                  gt_ref, sh_ref, sc_ref, g_ref, wr_ref, br_ref,
                  x1_ref, h2_ref, ti_ref, tw_ref):
    d = x_ref.shape[2]
    n_exp = wr_ref.shape[0]
    nq = y_ref.shape[1]
    y = (jnp.concatenate([y_ref[0, q] for q in range(nq)], axis=1)
         + dsk_ref[...] * jnp.concatenate([u_ref[0, q] for q in range(nq)], axis=1))
    a = jax.nn.gelu(y, approximate=True).astype(BF16)
    glu = jnp.dot(a, wglu_ref[...], preferred_element_type=F32)
    y_a = glu[:, :d] * jax.nn.sigmoid(glu[:, d:])
    y_b = jnp.concatenate(
        [lax.dot_general(ot_ref[0, cb], wo_ref[...], (((0,), (0,)), ((), ())), preferred_element_type=F32)
         for cb in range(ot_ref.shape[1])], axis=0)
    merged = ga_ref[0].astype(F32) * y_a + gb_ref[0].astype(F32) * y_b
    z = jnp.dot(merged.astype(BF16), wout_ref[...], preferred_element_type=F32)
    x1 = x_ref[0] + gt_ref[0] * z
    x1_ref[0] = x1
    r = lax.rsqrt(jnp.mean(x1 * x1, axis=-1, keepdims=True) + RMS_EPS)
    h2 = (x1 * r * g_ref[...]) * (1.0 + sc_ref[0]) + sh_ref[0]
    h2_ref[0] = h2
    logits = lax.dot_general(wr_ref[...], h2.astype(BF16), (((1,), (1,)), ((), ())),
                             preferred_element_type=F32) + br_ref[...]
    eid = lax.broadcasted_iota(jnp.int32, (n_exp, 1), 0)
    vals, idxs = [], []
    g = logits
    for _ in range(TOP_K):
        m = jnp.max(g, axis=0, keepdims=True)
        first = jnp.min(jnp.where(g == m, eid, n_exp), axis=0, keepdims=True)
        vals.append(m)
        idxs.append(first)
        g = jnp.where(eid == first, -jnp.inf, g)
    v = jnp.concatenate(vals, axis=0)
    e = jnp.exp(v - v[0:1])
    tw_ref[0] = e / jnp.sum(e, axis=0, keepdims=True)
    ti_ref[0] = jnp.concatenate(idxs, axis=0)


def merge(x, y_ssm, u, d_skip, ot, ga, gb, w_glu, w_attn_o, w_out, gate, shift, scale, g, w_router, b_router,
          tm=256):
    bsz, seq, d = x.shape
    nq = u.shape[1]
    sw = nq * LANES
    aw = ot.shape[2]
    n_exp = w_router.shape[1]
    tm = min(tm, seq)
    tok = lambda w: pl.BlockSpec((1, tm, w), lambda b, i: (b, i, 0))
    tiles = pl.BlockSpec((1, nq, tm, LANES), lambda b, i: (b, 0, i, 0))
    vec = pl.BlockSpec((1, 1, d), lambda b, i: (b, 0, 0))
    sel = pl.BlockSpec((1, TOP_K, tm), lambda b, i: (b, 0, i))
    return pl.pallas_call(
        _merge_kernel,
        grid=(bsz, seq // tm),
        in_specs=[tok(d), tiles, tiles, _const_spec((1, sw)),
                  pl.BlockSpec((1, tm // MOBA_BLOCK, aw, MOBA_BLOCK), lambda b, i: (b, i, 0, 0)),
                  tok(d), tok(d),
                  _const_spec(w_glu.shape), _const_spec(w_attn_o.shape), _const_spec(w_out.shape),
                  vec, vec, vec, _const_spec((1, d)), _const_spec((n_exp, d)), _const_spec((n_exp, 1))],
        out_specs=[tok(d), tok(d), sel, sel],
        out_shape=[jax.ShapeDtypeStruct((bsz, seq, d), F32),
                   jax.ShapeDtypeStruct((bsz, seq, d), F32),
                   jax.ShapeDtypeStruct((bsz, TOP_K, seq), jnp.int32),
                   jax.ShapeDtypeStruct((bsz, TOP_K, seq), F32)],
        compiler_params=_cparams(),
        name="merge",
    )(x, y_ssm, u, d_skip[None], ot, ga, gb, w_glu.astype(BF16), w_attn_o.astype(BF16), w_out.astype(BF16),
      gate[:, None], shift[:, None], scale[:, None], g[None], w_router.T.astype(BF16), b_router[:, None])


MOE_CHUNK = 256


def _moe_kernel(blk_e_ref, tok_hbm, dst_hbm, w_ref, h_hbm, wgu_ref, bgu_ref, wd_ref, bd_ref,
                out_hbm, tok_s, dst_s, xbuf, ybuf, xb_ref, act_ref, w1_ref, w2_ref, sem_idx, sem_g, sem_s,
                *, n_blocks):
    i = pl.program_id(0)
    rows = xbuf.shape[1]
    f = wd_ref.shape[1]

    def tok_copy(b, s):
        return pltpu.make_async_copy(tok_hbm.at[b], tok_s.at[s], sem_idx.at[s])

    def dst_copy(b, s):
        return pltpu.make_async_copy(dst_hbm.at[b], dst_s.at[s], sem_idx.at[2 + s])

    def gather_row(r, s):
        return pltpu.make_async_copy(h_hbm.at[pl.ds(tok_s[s, 0, r], 1)], xbuf.at[s, pl.ds(r, 1)], sem_g.at[s])

    def scatter_row(r, s):
        return pltpu.make_async_copy(ybuf.at[s, pl.ds(r, 1)], out_hbm.at[pl.ds(dst_s[s, 0, r], 1)], sem_s.at[s])

    def gather_all(s):
        return pltpu.make_async_copy(h_hbm.at[pl.ds(0, rows)], xbuf.at[s], sem_g.at[s])

    def scatter_all(s):
        return pltpu.make_async_copy(ybuf.at[s], out_hbm.at[pl.ds(0, rows)], sem_s.at[s])

    @pl.when(i == 0)
    def _():
        tok_copy(0, 0).start()
        tok_copy(0, 0).wait()

        def first_rows(r, carry):
            gather_row(r, 0).start()
            return carry

        lax.fori_loop(0, rows, first_rows, 0)
        tok_copy(1, 1).start()
        dst_copy(n_blocks, 1).start()
        ybuf[1] = jnp.zeros(ybuf.shape[1:], F32)

    def step(slot):
        oslot = 1 - slot
        tok_copy(0, oslot).wait()
        dst_copy(0, oslot).wait()
        tok_copy(i + 2, slot).start()
        dst_copy(i, slot).start()
        for r in range(rows):
            scatter_row(r, oslot).start(priority=r % 2)
        gather_all(slot).wait()

        @pl.when((i == 0) | (blk_e_ref[i] != blk_e_ref[jnp.maximum(i - 1, 0)]))
        def _():
            w1_ref[...] = wgu_ref[0].astype(BF16)
            w2_ref[...] = wd_ref[0].astype(BF16)

        xb_ref[...] = xbuf[slot].astype(BF16)
        for r in range(rows):
            gather_row(r, oslot).start(priority=r % 2)

        for c in range(f // MOE_CHUNK):
            lo, hi = c * MOE_CHUNK, (c + 1) * MOE_CHUNK
            xb = xb_ref[...]
            gate = jnp.dot(xb, w1_ref[:, lo:hi], preferred_element_type=F32) + bgu_ref[0, :, lo:hi]
            up = (jnp.dot(xb, w1_ref[:, f + lo:f + hi], preferred_element_type=F32)
                  + bgu_ref[0, :, f + lo:f + hi])
            gate = jnp.minimum(gate, SWIGLU_LIMIT)
            up = jnp.clip(up, -SWIGLU_LIMIT, SWIGLU_LIMIT)
            act_ref[:, lo:hi] = ((up + 1.0) * gate * jax.nn.sigmoid(SWIGLU_ALPHA * gate)).astype(BF16)
        scatter_all(oslot).wait()
        ybuf[slot] = (jnp.dot(act_ref[...], w2_ref[...], preferred_element_type=F32) + bd_ref[0]) * w_ref[0]

    for s in range(2):
        pl.when(i % 2 == s)(functools.partial(step, s))

    @pl.when(i == n_blocks - 1)
    def _():
        slot = (n_blocks - 1) % 2
        tok_copy(0, slot).wait()
        dst_copy(0, slot).wait()
        gather_all(1 - slot).wait()

        def last_rows(r, carry):
            scatter_row(r, slot).start()
            return carry

        lax.fori_loop(0, rows, last_rows, 0)
        scatter_all(slot).wait()


def moe_experts(h2, top_idx, top_w, w_gu, b_gu, w_down, b_down):
    n, d = h2.shape
    n_exp, _, f2 = w_gu.shape
    f = f2 // 2
    rb = EXPERT_ROWS
    nk = n * TOP_K
    i32 = jnp.int32
    flat_e = top_idx.reshape(nk)
    flat_w = top_w.reshape(nk)
    order = jnp.argsort(flat_e).astype(i32)
    counts = jnp.sum((flat_e[:, None] == jnp.arange(n_exp, dtype=i32)[None, :]).astype(i32), axis=0)
    group_start = jnp.cumsum(counts) - counts
    padded = (counts + rb - 1) // rb * rb
    padded_end = jnp.cumsum(padded)
    padded_start = padded_end - padded
    n_blocks = -(-nk // rb) + n_exp
    rows_total = n_blocks * rb
    blk_start = jnp.arange(n_blocks, dtype=i32) * rb
    blk_e = jnp.minimum(jnp.sum((padded_end[None, :] <= blk_start[:, None]).astype(i32), axis=1), n_exp - 1)
    row = jnp.arange(rows_total, dtype=i32)
    row_e = jnp.repeat(blk_e, rb)
    local = row - padded_start[row_e]
    valid = local < counts[row_e]
    pair = order[jnp.clip(group_start[row_e] + local, 0, nk - 1)]
    row_tok = jnp.where(valid, pair % n, 0)
    row_w = jnp.where(valid, flat_w[pair], 0.0)
    pad_rank = jnp.cumsum((~valid).astype(i32)) - 1
    row_dst = jnp.where(valid, pair, nk + pad_rank)
    tok_rows = jnp.concatenate([row_tok, jnp.zeros((2 * rb,), i32)]).reshape(n_blocks + 2, 1, rb)
    dst_rows = jnp.concatenate([row_dst, rows_total + jnp.arange(rb, dtype=i32)]).reshape(n_blocks + 1, 1, rb)

    grid_spec = pltpu.PrefetchScalarGridSpec(
        num_scalar_prefetch=1,
        grid=(n_blocks,),
        in_specs=[pl.BlockSpec(memory_space=pl.ANY),
                  pl.BlockSpec(memory_space=pl.ANY),
                  pl.BlockSpec((1, rb, 1), lambda i, be: (i, 0, 0)),
                  pl.BlockSpec(memory_space=pl.ANY),
                  pl.BlockSpec((1, d, f2), lambda i, be: (be[i], 0, 0)),
                  pl.BlockSpec((1, 1, f2), lambda i, be: (be[i], 0, 0)),
                  pl.BlockSpec((1, f, d), lambda i, be: (be[i], 0, 0)),
                  pl.BlockSpec((1, 1, d), lambda i, be: (be[i], 0, 0))],
        out_specs=pl.BlockSpec(memory_space=pl.ANY),
        scratch_shapes=[pltpu.SMEM((2, 1, rb), i32), pltpu.SMEM((2, 1, rb), i32),
                        pltpu.VMEM((2, rb, d), F32), pltpu.VMEM((2, rb, d), F32),
                        pltpu.VMEM((rb, d), BF16), pltpu.VMEM((rb, f), BF16),
                        pltpu.VMEM((d, f2), BF16), pltpu.VMEM((f, d), BF16),
                        pltpu.SemaphoreType.DMA((4,)), pltpu.SemaphoreType.DMA((2,)),
                        pltpu.SemaphoreType.DMA((2,))])
    return pl.pallas_call(
        functools.partial(_moe_kernel, n_blocks=n_blocks),
        grid_spec=grid_spec,
        out_shape=jax.ShapeDtypeStruct((rows_total + rb, d), F32),
        compiler_params=_cparams(),
        name="moe_experts",
    )(blk_e, tok_rows, dst_rows, row_w.reshape(n_blocks, rb, 1),
      h2, w_gu, b_gu[:, None], w_down, b_down[:, None])


def _combine_kernel(x_ref, *refs):
    y_refs, (gt_ref, g_ref, o_ref) = refs[:TOP_K], refs[TOP_K:]
    moe = y_refs[0][...]
    for y_ref in y_refs[1:]:
        moe = moe + y_ref[...]
    x2 = x_ref[0] + gt_ref[0] * moe
    r = lax.rsqrt(jnp.mean(x2 * x2, axis=-1, keepdims=True) + RMS_EPS)
    o_ref[0] = x2 * r * g_ref[...]


def combine(x1, ytk, gate, g, tm=256):
    bsz, seq, d = x1.shape
    tm = min(tm, seq)
    nt = seq // tm
    choice = lambda k: pl.BlockSpec((tm, d), lambda b, i: ((k * bsz + b) * nt + i, 0))
    return pl.pallas_call(
        _combine_kernel,
        grid=(bsz, nt),
        in_specs=[pl.BlockSpec((1, tm, d), lambda b, i: (b, i, 0))] + [choice(k) for k in range(TOP_K)]
                 + [pl.BlockSpec((1, 1, d), lambda b, i: (b, 0, 0)), _const_spec((1, d))],
        out_specs=pl.BlockSpec((1, tm, d), lambda b, i: (b, i, 0)),
        out_shape=jax.ShapeDtypeStruct((bsz, seq, d), F32),
        compiler_params=_cparams(),
        name="combine",
    )(x1, *([ytk] * TOP_K), gate[:, None], g[None])


def kernel(x, c, w_ada, b_ada, g_mix, w_in, ssm_a_re, ssm_a_im, ssm_log_dt, ssm_b_re, ssm_b_im, ssm_c_re,
           ssm_c_im, ssm_d, w_glu, w_attn_o, w_out, g_ffn, w_router, b_router, w_gu, b_gu, w_down, b_down,
           g_final):
    bsz, seq, d = x.shape
    depth = w_ada.shape[0]
    assert depth == 1
    sw = ssm_d.shape[1]
    aw = w_attn_o.shape[1]
    assert seq % MOBA_BLOCK == 0 and seq % SSM_CHUNK == 0
    for l in range(depth):
        mod = ada_mod(c, w_ada[l], b_ada[l])
        sh1, sc1, gt1, sh2, sc2, gt2 = jnp.split(mod, 6, axis=-1)
        u, k, kmean, ga, gb, qt, vt = in_proj(x, sh1, sc1, g_mix[l], w_in[l], sw, aw)
        ot = moba_attn(qt, k, vt, kmean.reshape(bsz, seq // MOBA_BLOCK, aw))
        y_ssm = ssm_mixer(u, ssm_a_re[l], ssm_a_im[l], ssm_log_dt[l], ssm_b_re[l], ssm_b_im[l],
                          ssm_c_re[l], ssm_c_im[l])
        x1, h2, top_idx, top_w = merge(x, y_ssm, u, ssm_d[l], ot, ga, gb, w_glu[l], w_attn_o[l], w_out[l],
                                       gt1, sh2, sc2, g_ffn[l], w_router[l], b_router[l])
        n = bsz * seq
        ytk = moe_experts(h2.reshape(n, d), top_idx.transpose(1, 0, 2).reshape(TOP_K, n),
                          top_w.transpose(1, 0, 2).reshape(TOP_K, n), w_gu[l], b_gu[l], w_down[l], b_down[l])
        x = combine(x1, ytk, gt2, g_final)
    return x
```

```python
import functools
import math

import jax
import jax.numpy as jnp
import numpy as np
from jax import lax
from jax.experimental import pallas as pl
from jax.experimental.pallas import tpu as pltpu

F32 = jnp.float32
BF16 = jnp.bfloat16

SSM_GROUP = 16
SSM_CHUNK = 16
HEAD_DIM = 64
MOBA_BLOCK = 256
MOBA_TOPK = 3
ROPE_THETA = 500000.0
ROT_DIM = HEAD_DIM // 4
QK_SCALE = HEAD_DIM ** -0.5 * math.log2(math.e)
TOP_K = 4
SWIGLU_LIMIT = 7.0
SWIGLU_ALPHA = 1.702
EXPERT_ROWS = 256
RMS_EPS = 1e-5
LANES = 128
SUBLANES = 8
NEG = -1e30
VMEM_LIMIT = 56 * 1024 * 1024


def _cparams(**kw):
    return pltpu.CompilerParams(vmem_limit_bytes=VMEM_LIMIT, **kw)


def _const_spec(shape):
    nd = len(shape)
    return pl.BlockSpec(shape, lambda *_: (0,) * nd, pipeline_mode=pl.Buffered(1))


def _ada_kernel(c_ref, w_ref, b_ref, o_ref):
    c = c_ref[...]
    cond = c * jax.nn.sigmoid(c)
    o_ref[...] = jnp.dot(cond, w_ref[...], preferred_element_type=F32) + b_ref[...]


def ada_mod(c, w, b):
    bsz, d = c.shape
    n = w.shape[1]
    tn = min(n, 1024)
    return pl.pallas_call(
        _ada_kernel,
        grid=(n // tn,),
        in_specs=[pl.BlockSpec((bsz, d), lambda j: (0, 0)),
                  pl.BlockSpec((d, tn), lambda j: (0, j)),
                  pl.BlockSpec((1, tn), lambda j: (0, j))],
        out_specs=pl.BlockSpec((bsz, tn), lambda j: (0, j)),
        out_shape=jax.ShapeDtypeStruct((bsz, n), F32),
        compiler_params=_cparams(),
        name="ada_mod",
    )(c, w, b[None])


def _rope_tables(seq):
    half = ROT_DIM // 2
    inv_freq = 1.0 / (ROPE_THETA ** (jnp.arange(0, ROT_DIM, 2, dtype=F32) / ROT_DIM))
    ang = jnp.arange(seq, dtype=F32)[:, None] * inv_freq[None, :]
    cos, sin = jnp.cos(ang), jnp.sin(ang)
    d = np.arange(LANES) % HEAD_DIM
    rot = jnp.asarray(d < ROT_DIM)
    cos_f = jnp.tile(cos, (1, LANES // half))
    sin_f = jnp.tile(sin, (1, LANES // half))
    cos_l = jnp.where(rot, cos_f, 1.0)
    sin_a = jnp.where(jnp.asarray(d < half), -sin_f, 0.0)
    sin_b = jnp.where(jnp.asarray((d >= half) & (d < ROT_DIM)), sin_f, 0.0)
    return cos_l, sin_a, sin_b, cos.T, sin.T


def _inproj_kernel(x_ref, sh_ref, sc_ref, g_ref, wm_ref, wt_ref, cl_ref, sa_ref, sb_ref, ct_ref, st_ref,
                   u_ref, k_ref, km_ref, ga_ref, gb_ref, qt_ref, vt_ref, *, sw, aw):
    tm, d = x_ref.shape[1], x_ref.shape[2]
    half = ROT_DIM // 2
    xf = x_ref[0]
    r = lax.rsqrt(jnp.mean(xf * xf, axis=-1, keepdims=True) + RMS_EPS)
    h = (xf * r * g_ref[...]) * (1.0 + sc_ref[0]) + sh_ref[0]
    hb = h.astype(BF16)
    pm = jnp.dot(hb, wm_ref[...], preferred_element_type=F32)
    for q in range(sw // LANES):
        u_ref[0, q] = pm[:, q * LANES:(q + 1) * LANES]
    ga_ref[0] = jax.nn.sigmoid(pm[:, sw + aw:sw + aw + d]).astype(BF16)
    gb_ref[0] = jax.nn.sigmoid(pm[:, sw + aw + d:]).astype(BF16)

    cl, sa, sb = cl_ref[...], sa_ref[...], sb_ref[...]
    parts = []
    for t in range(aw // LANES):
        kt = pm[:, sw + t * LANES:sw + (t + 1) * LANES]
        parts.append(kt * cl + pltpu.roll(kt, LANES - half, 1) * sa + pltpu.roll(kt, half, 1) * sb)
    krot = jnp.concatenate(parts, axis=1)
    k_ref[0] = krot.astype(BF16)
    km_ref[0, 0] = jnp.mean(krot.reshape(tm // MOBA_BLOCK, MOBA_BLOCK, aw), axis=1)

    qv = lax.dot_general(wt_ref[...], hb, (((1,), (1,)), ((), ())), preferred_element_type=F32)
    ct, st = ct_ref[...], st_ref[...]
    rows = []
    for hd in range(aw // HEAD_DIM):
        base = hd * HEAD_DIM
        t1 = qv[base:base + half]
        t2 = qv[base + half:base + ROT_DIM]
        rows += [t1 * ct - t2 * st, t2 * ct + t1 * st, qv[base + ROT_DIM:base + HEAD_DIM]]
    qb = (jnp.concatenate(rows, axis=0) * QK_SCALE).astype(BF16)
    vb = qv[aw:].astype(BF16)
    for cb in range(tm // MOBA_BLOCK):
        qt_ref[0, cb] = qb[:, cb * MOBA_BLOCK:(cb + 1) * MOBA_BLOCK]
        vt_ref[0, cb] = vb[:, cb * MOBA_BLOCK:(cb + 1) * MOBA_BLOCK]


def in_proj(x, shift, scale, g, w_in, sw, aw, tm=512):
    bsz, seq, d = x.shape
    tm = min(tm, seq)
    nb = seq // MOBA_BLOCK
    wm = jnp.concatenate([w_in[:, :sw], w_in[:, sw + aw:sw + 2 * aw], w_in[:, sw + 3 * aw:]], axis=1).astype(BF16)
    wt = jnp.concatenate([w_in[:, sw:sw + aw], w_in[:, sw + 2 * aw:sw + 3 * aw]], axis=1).T.astype(BF16)
    cos_l, sin_a, sin_b, cos_t, sin_t = _rope_tables(seq)
    tok = lambda w: pl.BlockSpec((1, tm, w), lambda b, i: (b, i, 0))
    vec = pl.BlockSpec((1, 1, d), lambda b, i: (b, 0, 0))
    tab = pl.BlockSpec((tm, LANES), lambda b, i: (i, 0))
    tabt = pl.BlockSpec((ROT_DIM // 2, tm), lambda b, i: (0, i))
    blkt = pl.BlockSpec((1, tm // MOBA_BLOCK, aw, MOBA_BLOCK), lambda b, i: (b, i, 0, 0))
    return pl.pallas_call(
        functools.partial(_inproj_kernel, sw=sw, aw=aw),
        grid=(bsz, seq // tm),
        in_specs=[tok(d), vec, vec, _const_spec((1, d)), _const_spec(wm.shape), _const_spec(wt.shape),
                  tab, tab, tab, tabt, tabt],
        out_specs=[pl.BlockSpec((1, sw // LANES, tm, LANES), lambda b, i: (b, 0, i, 0)), tok(aw),
                   pl.BlockSpec((1, 1, tm // MOBA_BLOCK, aw), lambda b, i: (b, i, 0, 0)),
                   tok(d), tok(d), blkt, blkt],
        out_shape=[jax.ShapeDtypeStruct((bsz, sw // LANES, seq, LANES), F32),
                   jax.ShapeDtypeStruct((bsz, seq, aw), BF16),
                   jax.ShapeDtypeStruct((bsz, seq // tm, tm // MOBA_BLOCK, aw), F32),
                   jax.ShapeDtypeStruct((bsz, seq, d), BF16),
                   jax.ShapeDtypeStruct((bsz, seq, d), BF16),
                   jax.ShapeDtypeStruct((bsz, nb, aw, MOBA_BLOCK), BF16),
                   jax.ShapeDtypeStruct((bsz, nb, aw, MOBA_BLOCK), BF16)],
        compiler_params=_cparams(),
        name="in_proj",
    )(x, shift[:, None], scale[:, None], g[None], wm, wt, cos_l, sin_a, sin_b, cos_t, sin_t)


ATTN_GROUP = 4
SCORE_ROWS = 32
ATTN_UNROLL = 2


def _attn_kernel(q_ref, k_ref, v_ref, km_ref, o_ref, qs_ref, bias_ref, s0_ref, s1_ref, p0_ref, p1_ref, acc_ref):
    i = pl.program_id(2)
    nb = km_ref.shape[1]
    blk = MOBA_BLOCK
    gw = ATTN_GROUP * HEAD_DIM
    n_slab = blk // SCORE_ROWS
    q4 = q_ref[0, 0]
    rowid = lax.broadcasted_iota(jnp.int32, (gw, 1), 0)
    kmb = km_ref[0].astype(BF16)
    bid = lax.broadcasted_iota(jnp.int32, (nb, 1), 0)
    for h in range(ATTN_GROUP):
        in_head = (rowid >= h * HEAD_DIM) & (rowid < (h + 1) * HEAD_DIM)
        qh = jnp.where(in_head, q4, jnp.zeros_like(q4))
        qs_ref[h] = qh
        gate = jnp.dot(kmb, qh, preferred_element_type=F32)
        g = jnp.where(bid < i, gate, -jnp.inf)
        bias = jnp.full(g.shape, NEG, F32)
        for _ in range(MOBA_TOPK):
            m = jnp.max(g, axis=0, keepdims=True)
            first = jnp.min(jnp.where(g == m, bid, nb), axis=0, keepdims=True)
            hit = (bid == first) & (m > -jnp.inf)
            bias = jnp.where(hit, 0.0, bias)
            g = jnp.where(hit, -jnp.inf, g)
        bias_ref[h] = bias

    kpos = lax.broadcasted_iota(jnp.int32, (SCORE_ROWS, blk), 0)
    qpos = lax.broadcasted_iota(jnp.int32, (SCORE_ROWS, blk), 1)

    def scores(j, s_ref):
        for h in range(ATTN_GROUP):
            s_ref[h] = jnp.dot(k_ref[0, pl.ds(pl.multiple_of(j * blk, blk), blk), :], qs_ref[h],
                               preferred_element_type=F32)

    def weighted_values(j, p_ref):
        return [jnp.dot(v_ref[0, j, h * HEAD_DIM:(h + 1) * HEAD_DIM, :], p_ref[h], preferred_element_type=F32)
                for h in range(ATTN_GROUP)]

    def block_update(j, s_ref, p_ref, ms, ls, own, nxt=None, pend=None):
        if nxt is not None:
            scores(*nxt)
        if pend is not None:
            pvs = weighted_values(pend[0], pend[1])
        ms_new, ls_new, scales = [], [], []
        for h in range(ATTN_GROUP):
            def slab(c):
                s = s_ref[h, c * SCORE_ROWS:(c + 1) * SCORE_ROWS, :]
                if own:
                    s = jnp.where(kpos + c * SCORE_ROWS <= qpos, s, NEG)
                return s
            cm = slab(0)
            for c in range(1, n_slab):
                cm = jnp.maximum(cm, slab(c))
            cm = jnp.max(cm, axis=0, keepdims=True)
            if own:
                mn, shift = cm, cm
                scales.append(jnp.zeros_like(cm))
            else:
                row = bias_ref[h, pl.ds(j, 1), :]
                mn = jnp.maximum(ms[h], cm + row)
                shift = mn - row
                scales.append(jnp.exp2(ms[h] - mn))
            psum = jnp.zeros((SCORE_ROWS, blk), F32)
            for c in range(n_slab):
                p = jnp.exp2(slab(c) - shift)
                psum = psum + p
                p_ref[h, c * SCORE_ROWS:(c + 1) * SCORE_ROWS, :] = p.astype(BF16)
            lsum = jnp.sum(psum, axis=0, keepdims=True)
            ms_new.append(mn)
            ls_new.append(lsum if own else scales[h] * ls[h] + lsum)
        if pend is not None:
            for h in range(ATTN_GROUP):
                acc_ref[h] = pend[2][h] * acc_ref[h] + pvs[h]
        return ms_new, ls_new, scales

    acc_ref[...] = jnp.zeros(acc_ref.shape, F32)
    scores(i, s1_ref)
    scores(0, s0_ref)
    ms, ls, sc = block_update(i, s1_ref, p1_ref, None, None, True)
    g = ATTN_GROUP

    bufs = ((s0_ref, p0_ref), (s1_ref, p1_ref))

    def body(t, carry):
        ms, ls, sc = carry[:g], carry[g:2 * g], carry[2 * g:]
        for un in range(ATTN_UNROLL):
            j = ATTN_UNROLL * t + un
            prev = jnp.where(j == 0, i, jnp.minimum(j - 1, i))
            s_cur, p_cur = bufs[un % 2]
            s_oth, p_oth = bufs[1 - un % 2]
            ms, ls, sc = block_update(jnp.minimum(j, i), s_cur, p_cur, ms, ls, False,
                                      nxt=(jnp.minimum(j + 1, i), s_oth), pend=(prev, p_oth, sc))
        return tuple(ms) + tuple(ls) + tuple(sc)

    n_trips = (i + ATTN_UNROLL - 1) // ATTN_UNROLL
    carry = lax.fori_loop(0, n_trips, body, tuple(ms) + tuple(ls) + tuple(sc))
    last = jnp.where(n_trips == 0, i, jnp.minimum(ATTN_UNROLL * n_trips - 1, i))
    pvs = weighted_values(last, p1_ref)
    for h in range(g):
        acc = carry[2 * g + h] * acc_ref[h] + pvs[h]
        o_ref[0, 0, h * HEAD_DIM:(h + 1) * HEAD_DIM, :] = (acc / carry[g + h]).astype(BF16)


def moba_attn(qt, k, vt, kmean):
    bsz, nb, aw, blk = qt.shape
    seq = k.shape[1]
    gw = ATTN_GROUP * HEAD_DIM
    return pl.pallas_call(
        _attn_kernel,
        grid=(bsz, aw // gw, nb),
        in_specs=[pl.BlockSpec((1, 1, gw, blk), lambda b, p, i: (b, i, p, 0)),
                  pl.BlockSpec((1, seq, gw), lambda b, p, i: (b, 0, p)),
                  pl.BlockSpec((1, nb, gw, blk), lambda b, p, i: (b, 0, p, 0)),
                  pl.BlockSpec((1, nb, gw), lambda b, p, i: (b, 0, p))],
        out_specs=pl.BlockSpec((1, 1, gw, blk), lambda b, p, i: (b, i, p, 0)),
        out_shape=jax.ShapeDtypeStruct((bsz, nb, aw, blk), BF16),
        scratch_shapes=[pltpu.VMEM((ATTN_GROUP, gw, blk), BF16),
                        pltpu.VMEM((ATTN_GROUP, nb, blk), F32),
                        pltpu.VMEM((ATTN_GROUP, blk, blk), F32),
                        pltpu.VMEM((ATTN_GROUP, blk, blk), F32),
                        pltpu.VMEM((ATTN_GROUP, blk, blk), BF16),
                        pltpu.VMEM((ATTN_GROUP, blk, blk), BF16),
                        pltpu.VMEM((ATTN_GROUP, HEAD_DIM, blk), F32)],
        compiler_params=_cparams(),
        name="moba_attn",
    )(qt, k, vt, kmean)


def _ssm_matrices(a_re, a_im, log_dt, b_re, b_im, c_re, c_im):
    grp, p_st = a_re.shape
    ch = b_re.shape[-1]
    gpt = LANES // ch
    nq = grp // gpt
    dt = jnp.exp(log_dt.astype(F32))[:, None]
    lr, li = a_re.astype(F32), a_im.astype(F32)

    def power(n):
        mag = jnp.exp(lr * dt * n)
        return mag * jnp.cos(li * dt * n), mag * jnp.sin(li * dt * n)

    abar_r, abar_i = power(1.0)
    at_r, at_i = power(float(SSM_CHUNK))
    den = lr * lr + li * li
    coef_r = ((abar_r - 1.0) * lr + abar_i * li) / den
    coef_i = (abar_i * lr - (abar_r - 1.0) * li) / den
    br, bi = b_re.astype(F32), b_im.astype(F32)
    bbar_r = coef_r[..., None] * br - coef_i[..., None] * bi
    bbar_i = coef_r[..., None] * bi + coef_i[..., None] * br
    eye = jnp.eye(gpt, dtype=F32)

    def diag_in(m):
        m = m.reshape(nq, gpt, p_st, ch)
        return jnp.einsum('qgpc,gh->qgchp', m, eye).reshape(nq, gpt * ch, gpt * p_st)

    def diag_out(m):
        m = m.reshape(nq, gpt, ch, p_st)
        return jnp.einsum('qgcp,gh->qgphc', m, eye).reshape(nq, gpt * p_st, gpt * ch)

    bd = jnp.concatenate([diag_in(bbar_r), diag_in(bbar_i)], axis=2)
    cd_r, cd_i = diag_out(c_re.astype(F32)), diag_out(-c_im.astype(F32))
    tile = lambda m: m.reshape(nq, 1, gpt * p_st)
    return tile(abar_r), tile(abar_i), at_r, at_i, bd, cd_r, cd_i


def _ssm_chunk_scan(u_ref, bd_ref, ar_ref, ai_ref, xr_ref, xi_ref, q, emit):
    n_rows, width = xr_ref.shape
    ar, ai = ar_ref[q], ai_ref[q]

    def body(j, carry):
        uj = u_ref[0, q, pl.ds(j, n_rows, stride=SSM_CHUNK), :].astype(BF16)
        bu = jnp.dot(uj, bd_ref[q], preferred_element_type=F32)
        xr, xi = xr_ref[...], xi_ref[...]
        nr = ar * xr - ai * xi + bu[:, :width]
        ni = ar * xi + ai * xr + bu[:, width:]
        xr_ref[...] = nr
        xi_ref[...] = ni
        if emit is not None:
            emit(j, nr, ni)
        return carry

    lax.fori_loop(0, SSM_CHUNK, body, 0)


def _ssm_state_kernel(u_ref, bd_ref, ar_ref, ai_ref, sr_ref, si_ref, xr_ref, xi_ref):
    width = xr_ref.shape[1]
    for q in range(bd_ref.shape[0]):
        xr_ref[...] = jnp.zeros(xr_ref.shape, F32)
        xi_ref[...] = jnp.zeros(xi_ref.shape, F32)
        _ssm_chunk_scan(u_ref, bd_ref, ar_ref, ai_ref, xr_ref, xi_ref, q, None)
        sr_ref[:, q * width:(q + 1) * width] = xr_ref[...]
        si_ref[:, q * width:(q + 1) * width] = xi_ref[...]


def _ssm_scan_kernel(sr_ref, si_ref, ar_ref, ai_ref, xr_ref, xi_ref):
    nc = sr_ref.shape[0]
    ar, ai = ar_ref[...], ai_ref[...]

    def body(c, carry):
        xr, xi = carry
        xr_ref[c] = xr
        xi_ref[c] = xi
        return (ar * xr - ai * xi + sr_ref[c], ar * xi + ai * xr + si_ref[c])

    zero = jnp.zeros(ar.shape, F32)
    lax.fori_loop(0, nc, body, (zero, zero), unroll=8)


def _ssm_out_kernel(u_ref, bd_ref, ar_ref, ai_ref, cr_ref, ci_ref, x0r_ref, x0i_ref, y_ref, xr_ref, xi_ref):
    n_rows, width = xr_ref.shape
    for q in range(bd_ref.shape[0]):
        xr_ref[...] = x0r_ref[:, q * width:(q + 1) * width]
        xi_ref[...] = x0i_ref[:, q * width:(q + 1) * width]

        def emit(j, xr, xi, q=q):
            y = (jnp.dot(xr.astype(BF16), cr_ref[q], preferred_element_type=F32)
                 + jnp.dot(xi.astype(BF16), ci_ref[q], preferred_element_type=F32))
            y_ref[0, q, pl.ds(j, n_rows, stride=SSM_CHUNK), :] = y

        _ssm_chunk_scan(u_ref, bd_ref, ar_ref, ai_ref, xr_ref, xi_ref, q, emit)


def ssm_mixer(u, a_re, a_im, log_dt, b_re, b_im, c_re, c_im, ts=4096):
    bsz, _, seq, _ = u.shape
    ts = min(ts, seq)
    nc = seq // SSM_CHUNK
    rows = ts // SSM_CHUNK
    abar_r, abar_i, at_r, at_i, bd, cd_r, cd_i = _ssm_matrices(a_re, a_im, log_dt, b_re, b_im, c_re, c_im)
    nq, _, width = abar_r.shape
    bd, cd_r, cd_i = bd.astype(BF16), cd_r.astype(BF16), cd_i.astype(BF16)
    ncols = bsz * nq * width
    u_spec = pl.BlockSpec((1, nq, ts, LANES), lambda b, i: (b, 0, i, 0))
    s_spec = pl.BlockSpec((rows, nq * width), lambda b, i: (i, b))
    consts = [_const_spec(bd.shape), _const_spec(abar_r.shape), _const_spec(abar_i.shape)]
    x_scratch = [pltpu.VMEM((rows, width), F32)] * 2
    s_re, s_im = pl.pallas_call(
        _ssm_state_kernel,
        grid=(bsz, seq // ts),
        in_specs=[u_spec] + consts,
        out_specs=[s_spec, s_spec],
        out_shape=[jax.ShapeDtypeStruct((nc, ncols), F32)] * 2,
        scratch_shapes=x_scratch,
        compiler_params=_cparams(),
        name="ssm_state",
    )(u, bd, abar_r, abar_i)

    sub = SUBLANES
    lw = ncols // sub
    a_r = jnp.tile(at_r.reshape(-1), bsz).reshape(sub, lw)
    a_i = jnp.tile(at_i.reshape(-1), bsz).reshape(sub, lw)
    sc_spec = pl.BlockSpec((nc, sub, LANES), lambda j: (0, 0, j))
    a_spec = pl.BlockSpec((sub, LANES), lambda j: (0, j))
    x_re, x_im = pl.pallas_call(
        _ssm_scan_kernel,
        grid=(lw // LANES,),
        in_specs=[sc_spec, sc_spec, a_spec, a_spec],
        out_specs=[sc_spec, sc_spec],
        out_shape=[jax.ShapeDtypeStruct((nc, sub, lw), F32)] * 2,
        compiler_params=_cparams(),
        name="ssm_scan",
    )(s_re.reshape(nc, sub, lw), s_im.reshape(nc, sub, lw), a_r, a_i)

    return pl.pallas_call(
        _ssm_out_kernel,
        grid=(bsz, seq // ts),
        in_specs=[u_spec] + consts + [_const_spec(cd_r.shape), _const_spec(cd_i.shape), s_spec, s_spec],
        out_specs=u_spec,
        out_shape=jax.ShapeDtypeStruct(u.shape, F32),
        scratch_shapes=x_scratch,
        compiler_params=_cparams(),
        name="ssm_out",
    )(u, bd, abar_r, abar_i, cd_r, cd_i, x_re.reshape(nc, ncols), x_im.reshape(nc, ncols))


def _merge_kernel(x_ref, y_ref, u_ref, dsk_ref, ot_ref, ga_ref, gb_ref, wglu_ref, wo_ref, wout_ref,
                  gt_ref, sh_ref, sc_ref, g_ref, wr_ref, br_ref,
                  x1_ref, h2_ref, ti_ref, tw_ref):
    d = x_ref.shape[2]
    n_exp = wr_ref.shape[0]
    nq = y_ref.shape[1]
    y = (jnp.concatenate([y_ref[0, q] for q in range(nq)], axis=1)
         + dsk_ref[...] * jnp.concatenate([u_ref[0, q] for q in range(nq)], axis=1))
    a = jax.nn.gelu(y, approximate=True).astype(BF16)
    glu = jnp.dot(a, wglu_ref[...], preferred_element_type=F32)
    y_a = glu[:, :d] * jax.nn.sigmoid(glu[:, d:])
    y_b = jnp.concatenate(
        [lax.dot_general(ot_ref[0, cb], wo_ref[...], (((0,), (0,)), ((), ())), preferred_element_type=F32)
         for cb in range(ot_ref.shape[1])], axis=0)
    merged = ga_ref[0].astype(F32) * y_a + gb_ref[0].astype(F32) * y_b
    z = jnp.dot(merged.astype(BF16), wout_ref[...], preferred_element_type=F32)
    x1 = x_ref[0] + gt_ref[0] * z
    x1_ref[0] = x1
    r = lax.rsqrt(jnp.mean(x1 * x1, axis=-1, keepdims=True) + RMS_EPS)
    h2 = (x1 * r * g_ref[...]) * (1.0 + sc_ref[0]) + sh_ref[0]
    h2_ref[0] = h2
    logits = lax.dot_general(wr_ref[...], h2.astype(BF16), (((1,), (1,)), ((), ())),
                             preferred_element_type=F32) + br_ref[...]
    eid = lax.broadcasted_iota(jnp.int32, (n_exp, 1), 0)
    vals, idxs = [], []
    g = logits
    for _ in range(TOP_K):
        m = jnp.max(g, axis=0, keepdims=True)
        first = jnp.min(jnp.where(g == m, eid, n_exp), axis=0, keepdims=True)
        vals.append(m)
        idxs.append(first)
        g = jnp.where(eid == first, -jnp.inf, g)
    v = jnp.concatenate(vals, axis=0)
    e = jnp.exp(v - v[0:1])
    tw_ref[0] = e / jnp.sum(e, axis=0, keepdims=True)
    ti_ref[0] = jnp.concatenate(idxs, axis=0)


def merge(x, y_ssm, u, d_skip, ot, ga, gb, w_glu, w_attn_o, w_out, gate, shift, scale, g, w_router, b_router,
          tm=256):
    bsz, seq, d = x.shape
    nq = u.shape[1]
    sw = nq * LANES
    aw = ot.shape[2]
    n_exp = w_router.shape[1]
    tm = min(tm, seq)
    tok = lambda w: pl.BlockSpec((1, tm, w), lambda b, i: (b, i, 0))
    tiles = pl.BlockSpec((1, nq, tm, LANES), lambda b, i: (b, 0, i, 0))
    vec = pl.BlockSpec((1, 1, d), lambda b, i: (b, 0, 0))
    sel = pl.BlockSpec((1, TOP_K, tm), lambda b, i: (b, 0, i))
    return pl.pallas_call(
        _merge_kernel,
        grid=(bsz, seq // tm),
        in_specs=[tok(d), tiles, tiles, _const_spec((1, sw)),
                  pl.BlockSpec((1, tm // MOBA_BLOCK, aw, MOBA_BLOCK), lambda b, i: (b, i, 0, 0)),
                  tok(d), tok(d),
                  _const_spec(w_glu.shape), _const_spec(w_attn_o.shape), _const_spec(w_out.shape),
                  vec, vec, vec, _const_spec((1, d)), _const_spec((n_exp, d)), _const_spec((n_exp, 1))],
        out_specs=[tok(d), tok(d), sel, sel],
        out_shape=[jax.ShapeDtypeStruct((bsz, seq, d), F32),
                   jax.ShapeDtypeStruct((bsz, seq, d), F32),
                   jax.ShapeDtypeStruct((bsz, TOP_K, seq), jnp.int32),
                   jax.ShapeDtypeStruct((bsz, TOP_K, seq), F32)],
        compiler_params=_cparams(),
        name="merge",
    )(x, y_ssm, u, d_skip[None], ot, ga, gb, w_glu.astype(BF16), w_attn_o.astype(BF16), w_out.astype(BF16),
      gate[:, None], shift[:, None], scale[:, None], g[None], w_router.T.astype(BF16), b_router[:, None])


MOE_CHUNK = 256


def _moe_kernel(blk_e_ref, tok_hbm, dst_hbm, w_ref, h_hbm, wgu_ref, bgu_ref, wd_ref, bd_ref,
                out_hbm, tok_s, dst_s, xbuf, ybuf, xb_ref, act_ref, w1_ref, w2_ref, sem_idx, sem_g, sem_s,
                *, n_blocks):
    i = pl.program_id(0)
    rows = xbuf.shape[1]
    f = wd_ref.shape[1]

    def tok_copy(b, s):
        return pltpu.make_async_copy(tok_hbm.at[b], tok_s.at[s], sem_idx.at[s])

    def dst_copy(b, s):
        return pltpu.make_async_copy(dst_hbm.at[b], dst_s.at[s], sem_idx.at[2 + s])

    def gather_row(r, s):
        return pltpu.make_async_copy(h_hbm.at[pl.ds(tok_s[s, 0, r], 1)], xbuf.at[s, pl.ds(r, 1)], sem_g.at[s])

    def scatter_row(r, s):
        return pltpu.make_async_copy(ybuf.at[s, pl.ds(r, 1)], out_hbm.at[pl.ds(dst_s[s, 0, r], 1)], sem_s.at[s])

    def gather_all(s):
        return pltpu.make_async_copy(h_hbm.at[pl.ds(0, rows)], xbuf.at[s], sem_g.at[s])

    def scatter_all(s):
        return pltpu.make_async_copy(ybuf.at[s], out_hbm.at[pl.ds(0, rows)], sem_s.at[s])

    @pl.when(i == 0)
    def _():
        tok_copy(0, 0).start()
        tok_copy(0, 0).wait()

        def first_rows(r, carry):
            gather_row(r, 0).start()
            return carry

        lax.fori_loop(0, rows, first_rows, 0)
        tok_copy(1, 1).start()
        dst_copy(n_blocks, 1).start()
        ybuf[1] = jnp.zeros(ybuf.shape[1:], F32)

    def step(slot):
        oslot = 1 - slot
        tok_copy(0, oslot).wait()
        dst_copy(0, oslot).wait()
        tok_copy(i + 2, slot).start()
        dst_copy(i, slot).start()
        gather_all(slot).wait()

        @pl.when((i == 0) | (blk_e_ref[i] != blk_e_ref[jnp.maximum(i - 1, 0)]))
        def _():
            w1_ref[...] = wgu_ref[0].astype(BF16)
            w2_ref[...] = wd_ref[0].astype(BF16)

        xb_ref[...] = xbuf[slot].astype(BF16)
        for r in range(rows):
            gather_row(r, oslot).start(priority=r % 2)

        for c in range(f // MOE_CHUNK):
            lo, hi = c * MOE_CHUNK, (c + 1) * MOE_CHUNK
            xb = xb_ref[...]
            gate = jnp.dot(xb, w1_ref[:, lo:hi], preferred_element_type=F32) + bgu_ref[0, :, lo:hi]
            up = (jnp.dot(xb, w1_ref[:, f + lo:f + hi], preferred_element_type=F32)
                  + bgu_ref[0, :, f + lo:f + hi])
            gate = jnp.minimum(gate, SWIGLU_LIMIT)
            up = jnp.clip(up, -SWIGLU_LIMIT, SWIGLU_LIMIT)
            act_ref[:, lo:hi] = ((up + 1.0) * gate * jax.nn.sigmoid(SWIGLU_ALPHA * gate)).astype(BF16)
        if slot == 0:
            pl.when(i > 0)(lambda: scatter_all(slot).wait())
        else:
            scatter_all(slot).wait()
        for r in range(rows):
            scatter_row(r, oslot).start(priority=r % 2)
        ybuf[slot] = (jnp.dot(act_ref[...], w2_ref[...], preferred_element_type=F32) + bd_ref[0]) * w_ref[0]

    for s in range(2):
        pl.when(i % 2 == s)(functools.partial(step, s))

    @pl.when(i == n_blocks - 1)
    def _():
        slot = (n_blocks - 1) % 2
        tok_copy(0, slot).wait()
        dst_copy(0, slot).wait()
        gather_all(1 - slot).wait()

        def last_rows(r, carry):
            scatter_row(r, slot).start()
            return carry

        lax.fori_loop(0, rows, last_rows, 0)
        scatter_all(1 - slot).wait()
        scatter_all(slot).wait()


def moe_experts(h2, top_idx, top_w, w_gu, b_gu, w_down, b_down):
    n, d = h2.shape
    n_exp, _, f2 = w_gu.shape
    f = f2 // 2
    rb = EXPERT_ROWS
    nk = n * TOP_K
    i32 = jnp.int32
    flat_e = top_idx.reshape(nk)
    flat_w = top_w.reshape(nk)
    order = jnp.argsort(flat_e).astype(i32)
    counts = jnp.sum((flat_e[:, None] == jnp.arange(n_exp, dtype=i32)[None, :]).astype(i32), axis=0)
    group_start = jnp.cumsum(counts) - counts
    padded = (counts + rb - 1) // rb * rb
    padded_end = jnp.cumsum(padded)
    padded_start = padded_end - padded
    n_blocks = -(-nk // rb) + n_exp
    rows_total = n_blocks * rb
    blk_start = jnp.arange(n_blocks, dtype=i32) * rb
    blk_e = jnp.minimum(jnp.sum((padded_end[None, :] <= blk_start[:, None]).astype(i32), axis=1), n_exp - 1)
    row = jnp.arange(rows_total, dtype=i32)
    row_e = jnp.repeat(blk_e, rb)
    local = row - padded_start[row_e]
    valid = local < counts[row_e]
    pair = order[jnp.clip(group_start[row_e] + local, 0, nk - 1)]
    row_tok = jnp.where(valid, pair % n, 0)
    row_w = jnp.where(valid, flat_w[pair], 0.0)
    pad_rank = jnp.cumsum((~valid).astype(i32)) - 1
    row_dst = jnp.where(valid, pair, nk + pad_rank)
    tok_rows = jnp.concatenate([row_tok, jnp.zeros((2 * rb,), i32)]).reshape(n_blocks + 2, 1, rb)
    dst_rows = jnp.concatenate([row_dst, rows_total + jnp.arange(rb, dtype=i32)]).reshape(n_blocks + 1, 1, rb)

    grid_spec = pltpu.PrefetchScalarGridSpec(
        num_scalar_prefetch=1,
        grid=(n_blocks,),
        in_specs=[pl.BlockSpec(memory_space=pl.ANY),
                  pl.BlockSpec(memory_space=pl.ANY),
                  pl.BlockSpec((1, rb, 1), lambda i, be: (i, 0, 0)),
                  pl.BlockSpec(memory_space=pl.ANY),
                  pl.BlockSpec((1, d, f2), lambda i, be: (be[i], 0, 0)),
                  pl.BlockSpec((1, 1, f2), lambda i, be: (be[i], 0, 0)),
                  pl.BlockSpec((1, f, d), lambda i, be: (be[i], 0, 0)),
                  pl.BlockSpec((1, 1, d), lambda i, be: (be[i], 0, 0))],
        out_specs=pl.BlockSpec(memory_space=pl.ANY),
        scratch_shapes=[pltpu.SMEM((2, 1, rb), i32), pltpu.SMEM((2, 1, rb), i32),
                        pltpu.VMEM((2, rb, d), F32), pltpu.VMEM((2, rb, d), F32),
                        pltpu.VMEM((rb, d), BF16), pltpu.VMEM((rb, f), BF16),
                        pltpu.VMEM((d, f2), BF16), pltpu.VMEM((f, d), BF16),
                        pltpu.SemaphoreType.DMA((4,)), pltpu.SemaphoreType.DMA((2,)),
                        pltpu.SemaphoreType.DMA((2,))])
    return pl.pallas_call(
        functools.partial(_moe_kernel, n_blocks=n_blocks),
        grid_spec=grid_spec,
        out_shape=jax.ShapeDtypeStruct((rows_total + rb, d), F32),
        compiler_params=_cparams(),
        name="moe_experts",
    )(blk_e, tok_rows, dst_rows, row_w.reshape(n_blocks, rb, 1),
      h2, w_gu, b_gu[:, None], w_down, b_down[:, None])


def _combine_kernel(x_ref, *refs):
    y_refs, (gt_ref, g_ref, o_ref) = refs[:TOP_K], refs[TOP_K:]
    moe = y_refs[0][...]
    for y_ref in y_refs[1:]:
        moe = moe + y_ref[...]
    x2 = x_ref[0] + gt_ref[0] * moe
    r = lax.rsqrt(jnp.mean(x2 * x2, axis=-1, keepdims=True) + RMS_EPS)
    o_ref[0] = x2 * r * g_ref[...]


def combine(x1, ytk, gate, g, tm=256):
    bsz, seq, d = x1.shape
    tm = min(tm, seq)
    nt = seq // tm
    choice = lambda k: pl.BlockSpec((tm, d), lambda b, i: ((k * bsz + b) * nt + i, 0))
    return pl.pallas_call(
        _combine_kernel,
        grid=(bsz, nt),
        in_specs=[pl.BlockSpec((1, tm, d), lambda b, i: (b, i, 0))] + [choice(k) for k in range(TOP_K)]
                 + [pl.BlockSpec((1, 1, d), lambda b, i: (b, 0, 0)), _const_spec((1, d))],
        out_specs=pl.BlockSpec((1, tm, d), lambda b, i: (b, i, 0)),
        out_shape=jax.ShapeDtypeStruct((bsz, seq, d), F32),
        compiler_params=_cparams(),
        name="combine",
    )(x1, *([ytk] * TOP_K), gate[:, None], g[None])


def kernel(x, c, w_ada, b_ada, g_mix, w_in, ssm_a_re, ssm_a_im, ssm_log_dt, ssm_b_re, ssm_b_im, ssm_c_re,
           ssm_c_im, ssm_d, w_glu, w_attn_o, w_out, g_ffn, w_router, b_router, w_gu, b_gu, w_down, b_down,
           g_final):
    bsz, seq, d = x.shape
    depth = w_ada.shape[0]
    assert depth == 1
    sw = ssm_d.shape[1]
    aw = w_attn_o.shape[1]
    assert seq % MOBA_BLOCK == 0 and seq % SSM_CHUNK == 0
    for l in range(depth):
        mod = ada_mod(c, w_ada[l], b_ada[l])
        sh1, sc1, gt1, sh2, sc2, gt2 = jnp.split(mod, 6, axis=-1)
        u, k, kmean, ga, gb, qt, vt = in_proj(x, sh1, sc1, g_mix[l], w_in[l], sw, aw)
        ot = moba_attn(qt, k, vt, kmean.reshape(bsz, seq // MOBA_BLOCK, aw))
        y_ssm = ssm_mixer(u, ssm_a_re[l], ssm_a_im[l], ssm_log_dt[l], ssm_b_re[l], ssm_b_im[l],
                          ssm_c_re[l], ssm_c_im[l])
        x1, h2, top_idx, top_w = merge(x, y_ssm, u, ssm_d[l], ot, ga, gb, w_glu[l], w_attn_o[l], w_out[l],
                                       gt1, sh2, sc2, g_ffn[l], w_router[l], b_router[l])
        n = bsz * seq
        ytk = moe_experts(h2.reshape(n, d), top_idx.transpose(1, 0, 2).reshape(TOP_K, n),
                          top_w.transpose(1, 0, 2).reshape(TOP_K, n), w_gu[l], b_gu[l], w_down[l], b_down[l])
        x = combine(x1, ytk, gt2, g_final)
    return x
```

```python
import functools
import math

import jax
import jax.numpy as jnp
import numpy as np
from jax import lax
from jax.experimental import pallas as pl
from jax.experimental.pallas import tpu as pltpu

F32 = jnp.float32
BF16 = jnp.bfloat16

SSM_GROUP = 16
SSM_CHUNK = 16
HEAD_DIM = 64
MOBA_BLOCK = 256
MOBA_TOPK = 3
ROPE_THETA = 500000.0
ROT_DIM = HEAD_DIM // 4
QK_SCALE = HEAD_DIM ** -0.5 * math.log2(math.e)
TOP_K = 4
SWIGLU_LIMIT = 7.0
SWIGLU_ALPHA = 1.702
EXPERT_ROWS = 256
RMS_EPS = 1e-5
LANES = 128
SUBLANES = 8
NEG = -1e30
VMEM_LIMIT = 56 * 1024 * 1024


def _cparams(**kw):
    return pltpu.CompilerParams(vmem_limit_bytes=VMEM_LIMIT, **kw)


def _const_spec(shape):
    nd = len(shape)
    return pl.BlockSpec(shape, lambda *_: (0,) * nd, pipeline_mode=pl.Buffered(1))


def _ada_kernel(c_ref, w_ref, b_ref, o_ref):
    c = c_ref[...]
    cond = c * jax.nn.sigmoid(c)
    o_ref[...] = jnp.dot(cond, w_ref[...], preferred_element_type=F32) + b_ref[...]


def ada_mod(c, w, b):
    bsz, d = c.shape
    n = w.shape[1]
    tn = min(n, 1024)
    return pl.pallas_call(
        _ada_kernel,
        grid=(n // tn,),
        in_specs=[pl.BlockSpec((bsz, d), lambda j: (0, 0)),
                  pl.BlockSpec((d, tn), lambda j: (0, j)),
                  pl.BlockSpec((1, tn), lambda j: (0, j))],
        out_specs=pl.BlockSpec((bsz, tn), lambda j: (0, j)),
        out_shape=jax.ShapeDtypeStruct((bsz, n), F32),
        compiler_params=_cparams(),
        name="ada_mod",
    )(c, w, b[None])


def _rope_tables(seq):
    half = ROT_DIM // 2
    inv_freq = 1.0 / (ROPE_THETA ** (jnp.arange(0, ROT_DIM, 2, dtype=F32) / ROT_DIM))
    pos = jnp.arange(seq, dtype=F32)
    d = np.arange(LANES) % HEAD_DIM
    ang_l = pos[:, None] * jnp.tile(inv_freq, LANES // half)[None, :]
    cos_f, sin_f = jnp.cos(ang_l), jnp.sin(ang_l)
    cos_l = jnp.where(jnp.asarray(d < ROT_DIM), cos_f, 1.0)
    sin_a = jnp.where(jnp.asarray(d < half), -sin_f, 0.0)
    sin_b = jnp.where(jnp.asarray((d >= half) & (d < ROT_DIM)), sin_f, 0.0)
    ang_t = inv_freq[:, None] * pos[None, :]
    return cos_l, sin_a, sin_b, jnp.cos(ang_t), jnp.sin(ang_t)


def _inproj_kernel(x_ref, sh_ref, sc_ref, g_ref, wm_ref, wt_ref, cl_ref, sa_ref, sb_ref, ct_ref, st_ref,
                   u_ref, k_ref, km_ref, ga_ref, gb_ref, qt_ref, vt_ref, *, sw, aw):
    tm, d = x_ref.shape[1], x_ref.shape[2]
    half = ROT_DIM // 2
    xf = x_ref[0]
    r = lax.rsqrt(jnp.mean(xf * xf, axis=-1, keepdims=True) + RMS_EPS)
    h = (xf * r * g_ref[...]) * (1.0 + sc_ref[0]) + sh_ref[0]
    hb = h.astype(BF16)
    pm = jnp.dot(hb, wm_ref[...], preferred_element_type=F32)
    for q in range(sw // LANES):
        u_ref[0, q] = pm[:, q * LANES:(q + 1) * LANES]
    ga_ref[0] = jax.nn.sigmoid(pm[:, sw + aw:sw + aw + d]).astype(BF16)
    gb_ref[0] = jax.nn.sigmoid(pm[:, sw + aw + d:]).astype(BF16)

    cl, sa, sb = cl_ref[...], sa_ref[...], sb_ref[...]
    parts = []
    for t in range(aw // LANES):
        kt = pm[:, sw + t * LANES:sw + (t + 1) * LANES]
        parts.append(kt * cl + pltpu.roll(kt, LANES - half, 1) * sa + pltpu.roll(kt, half, 1) * sb)
    krot = jnp.concatenate(parts, axis=1)
    k_ref[0] = krot.astype(BF16)
    km_ref[0, 0] = jnp.mean(krot.reshape(tm // MOBA_BLOCK, MOBA_BLOCK, aw), axis=1)

    qv = lax.dot_general(wt_ref[...], hb, (((1,), (1,)), ((), ())), preferred_element_type=F32)
    ct, st = ct_ref[...], st_ref[...]
    rows = []
    for hd in range(aw // HEAD_DIM):
        base = hd * HEAD_DIM
        t1 = qv[base:base + half]
        t2 = qv[base + half:base + ROT_DIM]
        rows += [t1 * ct - t2 * st, t2 * ct + t1 * st, qv[base + ROT_DIM:base + HEAD_DIM]]
    qb = (jnp.concatenate(rows, axis=0) * QK_SCALE).astype(BF16)
    vb = qv[aw:].astype(BF16)
    for cb in range(tm // MOBA_BLOCK):
        qt_ref[0, cb] = qb[:, cb * MOBA_BLOCK:(cb + 1) * MOBA_BLOCK]
        vt_ref[0, cb] = vb[:, cb * MOBA_BLOCK:(cb + 1) * MOBA_BLOCK]


def in_proj(x, shift, scale, g, w_in, sw, aw, tm=512):
    bsz, seq, d = x.shape
    tm = min(tm, seq)
    nb = seq // MOBA_BLOCK
    wm = jnp.concatenate([w_in[:, :sw], w_in[:, sw + aw:sw + 2 * aw], w_in[:, sw + 3 * aw:]], axis=1).astype(BF16)
    wt = jnp.concatenate([w_in[:, sw:sw + aw], w_in[:, sw + 2 * aw:sw + 3 * aw]], axis=1).T.astype(BF16)
    cos_l, sin_a, sin_b, cos_t, sin_t = _rope_tables(seq)
    tok = lambda w: pl.BlockSpec((1, tm, w), lambda b, i: (b, i, 0))
    vec = pl.BlockSpec((1, 1, d), lambda b, i: (b, 0, 0))
    tab = pl.BlockSpec((tm, LANES), lambda b, i: (i, 0))
    tabt = pl.BlockSpec((ROT_DIM // 2, tm), lambda b, i: (0, i))
    blkt = pl.BlockSpec((1, tm // MOBA_BLOCK, aw, MOBA_BLOCK), lambda b, i: (b, i, 0, 0))
    return pl.pallas_call(
        functools.partial(_inproj_kernel, sw=sw, aw=aw),
        grid=(bsz, seq // tm),
        in_specs=[tok(d), vec, vec, _const_spec((1, d)), _const_spec(wm.shape), _const_spec(wt.shape),
                  tab, tab, tab, tabt, tabt],
        out_specs=[pl.BlockSpec((1, sw // LANES, tm, LANES), lambda b, i: (b, 0, i, 0)), tok(aw),
                   pl.BlockSpec((1, 1, tm // MOBA_BLOCK, aw), lambda b, i: (b, i, 0, 0)),
                   tok(d), tok(d), blkt, blkt],
        out_shape=[jax.ShapeDtypeStruct((bsz, sw // LANES, seq, LANES), F32),
                   jax.ShapeDtypeStruct((bsz, seq, aw), BF16),
                   jax.ShapeDtypeStruct((bsz, seq // tm, tm // MOBA_BLOCK, aw), F32),
                   jax.ShapeDtypeStruct((bsz, seq, d), BF16),
                   jax.ShapeDtypeStruct((bsz, seq, d), BF16),
                   jax.ShapeDtypeStruct((bsz, nb, aw, MOBA_BLOCK), BF16),
                   jax.ShapeDtypeStruct((bsz, nb, aw, MOBA_BLOCK), BF16)],
        compiler_params=_cparams(),
        name="in_proj",
    )(x, shift[:, None], scale[:, None], g[None], wm, wt, cos_l, sin_a, sin_b, cos_t, sin_t)


ATTN_GROUP = 4
SCORE_ROWS = 32
ATTN_UNROLL = 2


def _attn_kernel(q_ref, k_ref, v_ref, km_ref, o_ref, qs_ref, bias_ref, s0_ref, s1_ref, p0_ref, p1_ref, acc_ref):
    i = pl.program_id(2)
    nb = km_ref.shape[1]
    blk = MOBA_BLOCK
    gw = ATTN_GROUP * HEAD_DIM
    n_slab = blk // SCORE_ROWS
    q4 = q_ref[0, 0]
    rowid = lax.broadcasted_iota(jnp.int32, (gw, 1), 0)
    kmb = km_ref[0].astype(BF16)
    bid = lax.broadcasted_iota(jnp.int32, (nb, 1), 0)
    for h in range(ATTN_GROUP):
        in_head = (rowid >= h * HEAD_DIM) & (rowid < (h + 1) * HEAD_DIM)
        qh = jnp.where(in_head, q4, jnp.zeros_like(q4))
        qs_ref[h] = qh
        gate = jnp.dot(kmb, qh, preferred_element_type=F32)
        g = jnp.where(bid < i, gate, -jnp.inf)
        bias = jnp.full(g.shape, NEG, F32)
        for _ in range(MOBA_TOPK):
            m = jnp.max(g, axis=0, keepdims=True)
            first = jnp.min(jnp.where(g == m, bid, nb), axis=0, keepdims=True)
            hit = (bid == first) & (m > -jnp.inf)
            bias = jnp.where(hit, 0.0, bias)
            g = jnp.where(hit, -jnp.inf, g)
        bias_ref[h] = bias

    kpos = lax.broadcasted_iota(jnp.int32, (SCORE_ROWS, blk), 0)
    qpos = lax.broadcasted_iota(jnp.int32, (SCORE_ROWS, blk), 1)

    def scores(j, s_ref):
        for h in range(ATTN_GROUP):
            s_ref[h] = jnp.dot(k_ref[0, pl.ds(pl.multiple_of(j * blk, blk), blk), :], qs_ref[h],
                               preferred_element_type=F32)

    def weighted_values(j, p_ref):
        return [jnp.dot(v_ref[0, j, h * HEAD_DIM:(h + 1) * HEAD_DIM, :], p_ref[h], preferred_element_type=F32)
                for h in range(ATTN_GROUP)]

    def block_update(j, s_ref, p_ref, ms, ls, own, nxt=None, pend=None):
        if nxt is not None:
            scores(*nxt)
        if pend is not None:
            pvs = weighted_values(pend[0], pend[1])
        ms_new, ls_new, scales = [], [], []
        for h in range(ATTN_GROUP):
            def slab(c):
                s = s_ref[h, c * SCORE_ROWS:(c + 1) * SCORE_ROWS, :]
                if own:
                    s = jnp.where(kpos + c * SCORE_ROWS <= qpos, s, NEG)
                return s
            cm = slab(0)
            for c in range(1, n_slab):
                cm = jnp.maximum(cm, slab(c))
            cm = jnp.max(cm, axis=0, keepdims=True)
            if own:
                mn, shift = cm, cm
                scales.append(jnp.zeros_like(cm))
            else:
                row = bias_ref[h, pl.ds(j, 1), :]
                mn = jnp.maximum(ms[h], cm + row)
                shift = mn - row
                scales.append(jnp.exp2(ms[h] - mn))
            psum = jnp.zeros((SCORE_ROWS, blk), F32)
            for c in range(n_slab):
                p = jnp.exp2(slab(c) - shift)
                psum = psum + p
                p_ref[h, c * SCORE_ROWS:(c + 1) * SCORE_ROWS, :] = p.astype(BF16)
            lsum = jnp.sum(psum, axis=0, keepdims=True)
            ms_new.append(mn)
            ls_new.append(lsum if own else scales[h] * ls[h] + lsum)
        if pend is not None:
            for h in range(ATTN_GROUP):
                acc_ref[h] = pend[2][h] * acc_ref[h] + pvs[h]
        return ms_new, ls_new, scales

    acc_ref[...] = jnp.zeros(acc_ref.shape, F32)
    scores(i, s1_ref)
    scores(0, s0_ref)
    ms, ls, sc = block_update(i, s1_ref, p1_ref, None, None, True)
    g = ATTN_GROUP

    bufs = ((s0_ref, p0_ref), (s1_ref, p1_ref))

    def body(t, carry):
        ms, ls, sc = carry[:g], carry[g:2 * g], carry[2 * g:]
        for un in range(ATTN_UNROLL):
            j = ATTN_UNROLL * t + un
            prev = jnp.where(j == 0, i, jnp.minimum(j - 1, i))
            s_cur, p_cur = bufs[un % 2]
            s_oth, p_oth = bufs[1 - un % 2]
            ms, ls, sc = block_update(jnp.minimum(j, i), s_cur, p_cur, ms, ls, False,
                                      nxt=(jnp.minimum(j + 1, i), s_oth), pend=(prev, p_oth, sc))
        return tuple(ms) + tuple(ls) + tuple(sc)

    n_trips = (i + ATTN_UNROLL - 1) // ATTN_UNROLL
    carry = lax.fori_loop(0, n_trips, body, tuple(ms) + tuple(ls) + tuple(sc))
    last = jnp.where(n_trips == 0, i, jnp.minimum(ATTN_UNROLL * n_trips - 1, i))
    pvs = weighted_values(last, p1_ref)
    for h in range(g):
        acc = carry[2 * g + h] * acc_ref[h] + pvs[h]
        o_ref[0, 0, h * HEAD_DIM:(h + 1) * HEAD_DIM, :] = (acc / carry[g + h]).astype(BF16)


def moba_attn(qt, k, vt, kmean):
    bsz, nb, aw, blk = qt.shape
    seq = k.shape[1]
    gw = ATTN_GROUP * HEAD_DIM
    return pl.pallas_call(
        _attn_kernel,
        grid=(bsz, aw // gw, nb),
        in_specs=[pl.BlockSpec((1, 1, gw, blk), lambda b, p, i: (b, i, p, 0)),
                  pl.BlockSpec((1, seq, gw), lambda b, p, i: (b, 0, p)),
                  pl.BlockSpec((1, nb, gw, blk), lambda b, p, i: (b, 0, p, 0)),
                  pl.BlockSpec((1, nb, gw), lambda b, p, i: (b, 0, p))],
        out_specs=pl.BlockSpec((1, 1, gw, blk), lambda b, p, i: (b, i, p, 0)),
        out_shape=jax.ShapeDtypeStruct((bsz, nb, aw, blk), BF16),
        scratch_shapes=[pltpu.VMEM((ATTN_GROUP, gw, blk), BF16),
                        pltpu.VMEM((ATTN_GROUP, nb, blk), F32),
                        pltpu.VMEM((ATTN_GROUP, blk, blk), F32),
                        pltpu.VMEM((ATTN_GROUP, blk, blk), F32),
                        pltpu.VMEM((ATTN_GROUP, blk, blk), BF16),
                        pltpu.VMEM((ATTN_GROUP, blk, blk), BF16),
                        pltpu.VMEM((ATTN_GROUP, HEAD_DIM, blk), F32)],
        compiler_params=_cparams(),
        name="moba_attn",
    )(qt, k, vt, kmean)


def _ssm_matrices(a_re, a_im, log_dt, b_re, b_im, c_re, c_im):
    grp, p_st = a_re.shape
    ch = b_re.shape[-1]
    gpt = LANES // ch
    nq = grp // gpt
    dt = jnp.exp(log_dt.astype(F32))[:, None]
    lr, li = a_re.astype(F32), a_im.astype(F32)

    def power(n):
        mag = jnp.exp(lr * dt * n)
        return mag * jnp.cos(li * dt * n), mag * jnp.sin(li * dt * n)

    abar_r, abar_i = power(1.0)
    at_r, at_i = power(float(SSM_CHUNK))
    den = lr * lr + li * li
    coef_r = ((abar_r - 1.0) * lr + abar_i * li) / den
    coef_i = (abar_i * lr - (abar_r - 1.0) * li) / den
    br, bi = b_re.astype(F32), b_im.astype(F32)
    bbar_r = coef_r[..., None] * br - coef_i[..., None] * bi
    bbar_i = coef_r[..., None] * bi + coef_i[..., None] * br
    eye = jnp.eye(gpt, dtype=F32)

    def diag_in(m):
        m = m.reshape(nq, gpt, p_st, ch)
        return jnp.einsum('qgpc,gh->qgchp', m, eye).reshape(nq, gpt * ch, gpt * p_st)

    def diag_out(m):
        m = m.reshape(nq, gpt, ch, p_st)
        return jnp.einsum('qgcp,gh->qgphc', m, eye).reshape(nq, gpt * p_st, gpt * ch)

    bd = jnp.concatenate([diag_in(bbar_r), diag_in(bbar_i)], axis=2)
    cd_r, cd_i = diag_out(c_re.astype(F32)), diag_out(-c_im.astype(F32))
    tile = lambda m: m.reshape(nq, 1, gpt * p_st)
    return tile(abar_r), tile(abar_i), at_r, at_i, bd, cd_r, cd_i


def _ssm_chunk_scan(u_ref, bd_ref, ar_ref, ai_ref, xr_ref, xi_ref, q, emit):
    n_rows, width = xr_ref.shape
    ar, ai = ar_ref[q], ai_ref[q]

    def body(j, carry):
        uj = u_ref[0, q, pl.ds(j, n_rows, stride=SSM_CHUNK), :].astype(BF16)
        bu = jnp.dot(uj, bd_ref[q], preferred_element_type=F32)
        xr, xi = xr_ref[...], xi_ref[...]
        nr = ar * xr - ai * xi + bu[:, :width]
        ni = ar * xi + ai * xr + bu[:, width:]
        xr_ref[...] = nr
        xi_ref[...] = ni
        if emit is not None:
            emit(j, nr, ni)
        return carry

    lax.fori_loop(0, SSM_CHUNK, body, 0)


def _ssm_state_kernel(u_ref, bd_ref, ar_ref, ai_ref, sr_ref, si_ref, xr_ref, xi_ref):
    width = xr_ref.shape[1]
    for q in range(bd_ref.shape[0]):
        xr_ref[...] = jnp.zeros(xr_ref.shape, F32)
        xi_ref[...] = jnp.zeros(xi_ref.shape, F32)
        _ssm_chunk_scan(u_ref, bd_ref, ar_ref, ai_ref, xr_ref, xi_ref, q, None)
        sr_ref[:, q * width:(q + 1) * width] = xr_ref[...]
        si_ref[:, q * width:(q + 1) * width] = xi_ref[...]


def _ssm_scan_kernel(sr_ref, si_ref, ar_ref, ai_ref, xr_ref, xi_ref):
    nc = sr_ref.shape[0]
    ar, ai = ar_ref[...], ai_ref[...]

    def body(c, carry):
        xr, xi = carry
        xr_ref[c] = xr
        xi_ref[c] = xi
        return (ar * xr - ai * xi + sr_ref[c], ar * xi + ai * xr + si_ref[c])

    zero = jnp.zeros(ar.shape, F32)
    lax.fori_loop(0, nc, body, (zero, zero), unroll=8)


def _ssm_out_kernel(u_ref, bd_ref, ar_ref, ai_ref, cr_ref, ci_ref, x0r_ref, x0i_ref, y_ref, xr_ref, xi_ref):
    n_rows, width = xr_ref.shape
    for q in range(bd_ref.shape[0]):
        xr_ref[...] = x0r_ref[:, q * width:(q + 1) * width]
        xi_ref[...] = x0i_ref[:, q * width:(q + 1) * width]

        def emit(j, xr, xi, q=q):
            y = (jnp.dot(xr.astype(BF16), cr_ref[q], preferred_element_type=F32)
                 + jnp.dot(xi.astype(BF16), ci_ref[q], preferred_element_type=F32))
            y_ref[0, q, pl.ds(j, n_rows, stride=SSM_CHUNK), :] = y

        _ssm_chunk_scan(u_ref, bd_ref, ar_ref, ai_ref, xr_ref, xi_ref, q, emit)


def ssm_mixer(u, a_re, a_im, log_dt, b_re, b_im, c_re, c_im, ts=4096):
    bsz, _, seq, _ = u.shape
    ts = min(ts, seq)
    nc = seq // SSM_CHUNK
    rows = ts // SSM_CHUNK
    abar_r, abar_i, at_r, at_i, bd, cd_r, cd_i = _ssm_matrices(a_re, a_im, log_dt, b_re, b_im, c_re, c_im)
    nq, _, width = abar_r.shape
    bd, cd_r, cd_i = bd.astype(BF16), cd_r.astype(BF16), cd_i.astype(BF16)
    ncols = bsz * nq * width
    u_spec = pl.BlockSpec((1, nq, ts, LANES), lambda b, i: (b, 0, i, 0))
    s_spec = pl.BlockSpec((rows, nq * width), lambda b, i: (i, b))
    consts = [_const_spec(bd.shape), _const_spec(abar_r.shape), _const_spec(abar_i.shape)]
    x_scratch = [pltpu.VMEM((rows, width), F32)] * 2
    s_re, s_im = pl.pallas_call(
        _ssm_state_kernel,
        grid=(bsz, seq // ts),
        in_specs=[u_spec] + consts,
        out_specs=[s_spec, s_spec],
        out_shape=[jax.ShapeDtypeStruct((nc, ncols), F32)] * 2,
        scratch_shapes=x_scratch,
        compiler_params=_cparams(),
        name="ssm_state",
    )(u, bd, abar_r, abar_i)

    sub = SUBLANES
    lw = ncols // sub
    a_r = jnp.tile(at_r.reshape(-1), bsz).reshape(sub, lw)
    a_i = jnp.tile(at_i.reshape(-1), bsz).reshape(sub, lw)
    sc_spec = pl.BlockSpec((nc, sub, LANES), lambda j: (0, 0, j))
    a_spec = pl.BlockSpec((sub, LANES), lambda j: (0, j))
    x_re, x_im = pl.pallas_call(
        _ssm_scan_kernel,
        grid=(lw // LANES,),
        in_specs=[sc_spec, sc_spec, a_spec, a_spec],
        out_specs=[sc_spec, sc_spec],
        out_shape=[jax.ShapeDtypeStruct((nc, sub, lw), F32)] * 2,
        compiler_params=_cparams(),
        name="ssm_scan",
    )(s_re.reshape(nc, sub, lw), s_im.reshape(nc, sub, lw), a_r, a_i)

    return pl.pallas_call(
        _ssm_out_kernel,
        grid=(bsz, seq // ts),
        in_specs=[u_spec] + consts + [_const_spec(cd_r.shape), _const_spec(cd_i.shape), s_spec, s_spec],
        out_specs=u_spec,
        out_shape=jax.ShapeDtypeStruct(u.shape, F32),
        scratch_shapes=x_scratch,
        compiler_params=_cparams(),
        name="ssm_out",
    )(u, bd, abar_r, abar_i, cd_r, cd_i, x_re.reshape(nc, ncols), x_im.reshape(nc, ncols))


def _merge_kernel(x_ref, y_ref, u_ref, dsk_ref, ot_ref, ga_ref, gb_ref, wglu_ref, wo_ref, wout_ref,
                  gt_ref, sh_ref, sc_ref, g_ref, wr_ref, br_ref,
                  x1_ref, h2_ref, ti_ref, tw_ref):
    d = x_ref.shape[2]
    n_exp = wr_ref.shape[0]
    nq = y_ref.shape[1]
    y = (jnp.concatenate([y_ref[0, q] for q in range(nq)], axis=1)
         + dsk_ref[...] * jnp.concatenate([u_ref[0, q] for q in range(nq)], axis=1))
    a = jax.nn.gelu(y, approximate=True).astype(BF16)
    glu = jnp.dot(a, wglu_ref[...], preferred_element_type=F32)
    y_a = glu[:, :d] * jax.nn.sigmoid(glu[:, d:])
    y_b = jnp.concatenate(
        [lax.dot_general(ot_ref[0, cb], wo_ref[...], (((0,), (0,)), ((), ())), preferred_element_type=F32)
         for cb in range(ot_ref.shape[1])], axis=0)
    merged = ga_ref[0].astype(F32) * y_a + gb_ref[0].astype(F32) * y_b
    z = jnp.dot(merged.astype(BF16), wout_ref[...], preferred_element_type=F32)
    x1 = x_ref[0] + gt_ref[0] * z
    x1_ref[0] = x1
    r = lax.rsqrt(jnp.mean(x1 * x1, axis=-1, keepdims=True) + RMS_EPS)
    h2 = (x1 * r * g_ref[...]) * (1.0 + sc_ref[0]) + sh_ref[0]
    h2_ref[0] = h2
    logits = lax.dot_general(wr_ref[...], h2.astype(BF16), (((1,), (1,)), ((), ())),
                             preferred_element_type=F32) + br_ref[...]
    eid = lax.broadcasted_iota(jnp.int32, (n_exp, 1), 0)
    vals, idxs = [], []
    g = logits
    for _ in range(TOP_K):
        m = jnp.max(g, axis=0, keepdims=True)
        first = jnp.min(jnp.where(g == m, eid, n_exp), axis=0, keepdims=True)
        vals.append(m)
        idxs.append(first)
        g = jnp.where(eid == first, -jnp.inf, g)
    v = jnp.concatenate(vals, axis=0)
    e = jnp.exp(v - v[0:1])
    tw_ref[0] = e / jnp.sum(e, axis=0, keepdims=True)
    ti_ref[0] = jnp.concatenate(idxs, axis=0)


def merge(x, y_ssm, u, d_skip, ot, ga, gb, w_glu, w_attn_o, w_out, gate, shift, scale, g, w_router, b_router,
          tm=256):
    bsz, seq, d = x.shape
    nq = u.shape[1]
    sw = nq * LANES
    aw = ot.shape[2]
    n_exp = w_router.shape[1]
    tm = min(tm, seq)
    tok = lambda w: pl.BlockSpec((1, tm, w), lambda b, i: (b, i, 0))
    tiles = pl.BlockSpec((1, nq, tm, LANES), lambda b, i: (b, 0, i, 0))
    vec = pl.BlockSpec((1, 1, d), lambda b, i: (b, 0, 0))
    sel = pl.BlockSpec((1, TOP_K, tm), lambda b, i: (b, 0, i))
    return pl.pallas_call(
        _merge_kernel,
        grid=(bsz, seq // tm),
        in_specs=[tok(d), tiles, tiles, _const_spec((1, sw)),
                  pl.BlockSpec((1, tm // MOBA_BLOCK, aw, MOBA_BLOCK), lambda b, i: (b, i, 0, 0)),
                  tok(d), tok(d),
                  _const_spec(w_glu.shape), _const_spec(w_attn_o.shape), _const_spec(w_out.shape),
                  vec, vec, vec, _const_spec((1, d)), _const_spec((n_exp, d)), _const_spec((n_exp, 1))],
        out_specs=[tok(d), tok(d), sel, sel],
        out_shape=[jax.ShapeDtypeStruct((bsz, seq, d), F32),
                   jax.ShapeDtypeStruct((bsz, seq, d), F32),
                   jax.ShapeDtypeStruct((bsz, TOP_K, seq), jnp.int32),
                   jax.ShapeDtypeStruct((bsz, TOP_K, seq), F32)],
        compiler_params=_cparams(),
        name="merge",
    )(x, y_ssm, u, d_skip[None], ot, ga, gb, w_glu.astype(BF16), w_attn_o.astype(BF16), w_out.astype(BF16),
      gate[:, None], shift[:, None], scale[:, None], g[None], w_router.T.astype(BF16), b_router[:, None])


MOE_CHUNK = 256


def _moe_kernel(blk_e_ref, tok_hbm, dst_hbm, w_ref, h_hbm, wgu_ref, bgu_ref, wd_ref, bd_ref,
                out_hbm, tok_s, dst_s, xbuf, ybuf, xb_ref, act_ref, w1_ref, w2_ref, sem_idx, sem_g, sem_s,
                *, n_blocks):
    i = pl.program_id(0)
    rows = xbuf.shape[1]
    f = wd_ref.shape[1]

    def tok_copy(b, s):
        return pltpu.make_async_copy(tok_hbm.at[b], tok_s.at[s], sem_idx.at[s])

    def dst_copy(b, s):
        return pltpu.make_async_copy(dst_hbm.at[b], dst_s.at[s], sem_idx.at[2 + s])

    def gather_row(r, s):
        return pltpu.make_async_copy(h_hbm.at[pl.ds(tok_s[s, 0, r], 1)], xbuf.at[s, pl.ds(r, 1)], sem_g.at[s])

    def scatter_row(r, s):
        return pltpu.make_async_copy(ybuf.at[s, pl.ds(r, 1)], out_hbm.at[pl.ds(dst_s[s, 0, r], 1)], sem_s.at[s])

    def gather_all(s):
        return pltpu.make_async_copy(h_hbm.at[pl.ds(0, rows)], xbuf.at[s], sem_g.at[s])

    def scatter_all(s):
        return pltpu.make_async_copy(ybuf.at[s], out_hbm.at[pl.ds(0, rows)], sem_s.at[s])

    @pl.when(i == 0)
    def _():
        tok_copy(0, 0).start()
        tok_copy(0, 0).wait()

        def first_rows(r, carry):
            gather_row(r, 0).start()
            return carry

        lax.fori_loop(0, rows, first_rows, 0)
        tok_copy(1, 1).start()
        dst_copy(n_blocks, 1).start()
        ybuf[1] = jnp.zeros(ybuf.shape[1:], F32)

    def step(slot):
        oslot = 1 - slot
        tok_copy(0, oslot).wait()
        dst_copy(0, oslot).wait()
        tok_copy(i + 2, slot).start()
        dst_copy(i, slot).start()
        gather_all(slot).wait()

        @pl.when((i == 0) | (blk_e_ref[i] != blk_e_ref[jnp.maximum(i - 1, 0)]))
        def _():
            w1_ref[...] = wgu_ref[0].astype(BF16)
            w2_ref[...] = wd_ref[0].astype(BF16)

        xb_ref[...] = xbuf[slot].astype(BF16)
        for r in range(rows):
            gather_row(r, oslot).start(priority=r % 2)

        for c in range(f // MOE_CHUNK):
            lo, hi = c * MOE_CHUNK, (c + 1) * MOE_CHUNK
            xb = xb_ref[...]
            gate = jnp.dot(xb, w1_ref[:, lo:hi], preferred_element_type=F32) + bgu_ref[0, :, lo:hi]
            up = (jnp.dot(xb, w1_ref[:, f + lo:f + hi], preferred_element_type=F32)
                  + bgu_ref[0, :, f + lo:f + hi])
            gate = jnp.minimum(gate, SWIGLU_LIMIT)
            up = jnp.clip(up, -SWIGLU_LIMIT, SWIGLU_LIMIT)
            act_ref[:, lo:hi] = ((up + 1.0) * gate * jax.nn.sigmoid(SWIGLU_ALPHA * gate)).astype(BF16)
        if slot == 0:
            pl.when(i > 0)(lambda: scatter_all(slot).wait())
        else:
            scatter_all(slot).wait()
        for r in range(rows):
            scatter_row(r, oslot).start(priority=r % 2)
        ybuf[slot] = (jnp.dot(act_ref[...], w2_ref[...], preferred_element_type=F32) + bd_ref[0]) * w_ref[0]

    for s in range(2):
        pl.when(i % 2 == s)(functools.partial(step, s))

    @pl.when(i == n_blocks - 1)
    def _():
        slot = (n_blocks - 1) % 2
        tok_copy(0, slot).wait()
        dst_copy(0, slot).wait()
        gather_all(1 - slot).wait()

        def last_rows(r, carry):
            scatter_row(r, slot).start()
            return carry

        lax.fori_loop(0, rows, last_rows, 0)
        scatter_all(1 - slot).wait()
        scatter_all(slot).wait()


def moe_experts(h2, top_idx, top_w, w_gu, b_gu, w_down, b_down):
    n, d = h2.shape
    n_exp, _, f2 = w_gu.shape
    f = f2 // 2
    rb = EXPERT_ROWS
    nk = n * TOP_K
    i32 = jnp.int32
    flat_e = top_idx.reshape(nk)
    flat_w = top_w.reshape(nk)
    order = jnp.argsort(flat_e).astype(i32)
    counts = jnp.sum((flat_e[:, None] == jnp.arange(n_exp, dtype=i32)[None, :]).astype(i32), axis=0)
    group_start = jnp.cumsum(counts) - counts
    padded = (counts + rb - 1) // rb * rb
    padded_end = jnp.cumsum(padded)
    padded_start = padded_end - padded
    n_blocks = -(-nk // rb) + n_exp
    rows_total = n_blocks * rb
    blk_start = jnp.arange(n_blocks, dtype=i32) * rb
    blk_e = jnp.minimum(jnp.sum((padded_end[None, :] <= blk_start[:, None]).astype(i32), axis=1), n_exp - 1)
    row = jnp.arange(rows_total, dtype=i32)
    row_e = jnp.repeat(blk_e, rb)
    local = row - padded_start[row_e]
    valid = local < counts[row_e]
    pair = order[jnp.clip(group_start[row_e] + local, 0, nk - 1)]
    row_tok = jnp.where(valid, pair % n, 0)
    row_w = jnp.where(valid, flat_w[pair], 0.0)
    pad_rank = jnp.cumsum((~valid).astype(i32)) - 1
    row_dst = jnp.where(valid, pair, nk + pad_rank)
    tok_rows = jnp.concatenate([row_tok, jnp.zeros((2 * rb,), i32)]).reshape(n_blocks + 2, 1, rb)
    dst_rows = jnp.concatenate([row_dst, rows_total + jnp.arange(rb, dtype=i32)]).reshape(n_blocks + 1, 1, rb)

    grid_spec = pltpu.PrefetchScalarGridSpec(
        num_scalar_prefetch=1,
        grid=(n_blocks,),
        in_specs=[pl.BlockSpec(memory_space=pl.ANY),
                  pl.BlockSpec(memory_space=pl.ANY),
                  pl.BlockSpec((1, rb, 1), lambda i, be: (i, 0, 0)),
                  pl.BlockSpec(memory_space=pl.ANY),
                  pl.BlockSpec((1, d, f2), lambda i, be: (be[i], 0, 0)),
                  pl.BlockSpec((1, 1, f2), lambda i, be: (be[i], 0, 0)),
                  pl.BlockSpec((1, f, d), lambda i, be: (be[i], 0, 0)),
                  pl.BlockSpec((1, 1, d), lambda i, be: (be[i], 0, 0))],
        out_specs=pl.BlockSpec(memory_space=pl.ANY),
        scratch_shapes=[pltpu.SMEM((2, 1, rb), i32), pltpu.SMEM((2, 1, rb), i32),
                        pltpu.VMEM((2, rb, d), F32), pltpu.VMEM((2, rb, d), F32),
                        pltpu.VMEM((rb, d), BF16), pltpu.VMEM((rb, f), BF16),
                        pltpu.VMEM((d, f2), BF16), pltpu.VMEM((f, d), BF16),
                        pltpu.SemaphoreType.DMA((4,)), pltpu.SemaphoreType.DMA((2,)),
                        pltpu.SemaphoreType.DMA((2,))])
    return pl.pallas_call(
        functools.partial(_moe_kernel, n_blocks=n_blocks),
        grid_spec=grid_spec,
        out_shape=jax.ShapeDtypeStruct((rows_total + rb, d), F32),
        compiler_params=_cparams(),
        name="moe_experts",
    )(blk_e, tok_rows, dst_rows, row_w.reshape(n_blocks, rb, 1),
      h2, w_gu, b_gu[:, None], w_down, b_down[:, None])


def _combine_kernel(x_ref, *refs):
    y_refs, (gt_ref, g_ref, o_ref) = refs[:TOP_K], refs[TOP_K:]
    moe = y_refs[0][...]
    for y_ref in y_refs[1:]:
        moe = moe + y_ref[...]
    x2 = x_ref[0] + gt_ref[0] * moe
    r = lax.rsqrt(jnp.mean(x2 * x2, axis=-1, keepdims=True) + RMS_EPS)
    o_ref[0] = x2 * r * g_ref[...]


def combine(x1, ytk, gate, g, tm=256):
    bsz, seq, d = x1.shape
    tm = min(tm, seq)
    nt = seq // tm
    choice = lambda k: pl.BlockSpec((tm, d), lambda b, i: ((k * bsz + b) * nt + i, 0))
    return pl.pallas_call(
        _combine_kernel,
        grid=(bsz, nt),
        in_specs=[pl.BlockSpec((1, tm, d), lambda b, i: (b, i, 0))] + [choice(k) for k in range(TOP_K)]
                 + [pl.BlockSpec((1, 1, d), lambda b, i: (b, 0, 0)), _const_spec((1, d))],
        out_specs=pl.BlockSpec((1, tm, d), lambda b, i: (b, i, 0)),
        out_shape=jax.ShapeDtypeStruct((bsz, seq, d), F32),
        compiler_params=_cparams(),
        name="combine",
    )(x1, *([ytk] * TOP_K), gate[:, None], g[None])


def kernel(x, c, w_ada, b_ada, g_mix, w_in, ssm_a_re, ssm_a_im, ssm_log_dt, ssm_b_re, ssm_b_im, ssm_c_re,
           ssm_c_im, ssm_d, w_glu, w_attn_o, w_out, g_ffn, w_router, b_router, w_gu, b_gu, w_down, b_down,
           g_final):
    bsz, seq, d = x.shape
    depth = w_ada.shape[0]
    assert depth == 1
    sw = ssm_d.shape[1]
    aw = w_attn_o.shape[1]
    assert seq % MOBA_BLOCK == 0 and seq % SSM_CHUNK == 0
    for l in range(depth):
        mod = ada_mod(c, w_ada[l], b_ada[l])
        sh1, sc1, gt1, sh2, sc2, gt2 = jnp.split(mod, 6, axis=-1)
        u, k, kmean, ga, gb, qt, vt = in_proj(x, sh1, sc1, g_mix[l], w_in[l], sw, aw)
        ot = moba_attn(qt, k, vt, kmean.reshape(bsz, seq // MOBA_BLOCK, aw))
        y_ssm = ssm_mixer(u, ssm_a_re[l], ssm_a_im[l], ssm_log_dt[l], ssm_b_re[l], ssm_b_im[l],
                          ssm_c_re[l], ssm_c_im[l])
        x1, h2, top_idx, top_w = merge(x, y_ssm, u, ssm_d[l], ot, ga, gb, w_glu[l], w_attn_o[l], w_out[l],
                                       gt1, sh2, sc2, g_ffn[l], w_router[l], b_router[l])
        n = bsz * seq
        ytk = moe_experts(h2.reshape(n, d), top_idx.transpose(1, 0, 2).reshape(TOP_K, n),
                          top_w.transpose(1, 0, 2).reshape(TOP_K, n), w_gu[l], b_gu[l], w_down[l], b_down[l])
        x = combine(x1, ytk, gt2, g_final)
    return x
```

```python
import functools
import math

import jax
import jax.numpy as jnp
import numpy as np
from jax import lax
from jax.experimental import pallas as pl
from jax.experimental.pallas import tpu as pltpu

F32 = jnp.float32
BF16 = jnp.bfloat16

SSM_GROUP = 16
SSM_CHUNK = 16
HEAD_DIM = 64
MOBA_BLOCK = 256
MOBA_TOPK = 3
ROPE_THETA = 500000.0
ROT_DIM = HEAD_DIM // 4
QK_SCALE = HEAD_DIM ** -0.5 * math.log2(math.e)
TOP_K = 4
SWIGLU_LIMIT = 7.0
SWIGLU_ALPHA = 1.702
EXPERT_ROWS = 256
RMS_EPS = 1e-5
LANES = 128
SUBLANES = 8
NEG = -1e30
VMEM_LIMIT = 56 * 1024 * 1024


def _cparams(**kw):
    return pltpu.CompilerParams(vmem_limit_bytes=VMEM_LIMIT, **kw)


def _const_spec(shape):
    nd = len(shape)
    return pl.BlockSpec(shape, lambda *_: (0,) * nd, pipeline_mode=pl.Buffered(1))


def _ada_kernel(c_ref, w_ref, b_ref, o_ref):
    c = c_ref[...]
    cond = c * jax.nn.sigmoid(c)
    o_ref[...] = jnp.dot(cond, w_ref[...], preferred_element_type=F32) + b_ref[...]


def ada_mod(c, w, b):
    bsz, d = c.shape
    n = w.shape[1]
    tn = min(n, 1024)
    return pl.pallas_call(
        _ada_kernel,
        grid=(n // tn,),
        in_specs=[pl.BlockSpec((bsz, d), lambda j: (0, 0)),
                  pl.BlockSpec((d, tn), lambda j: (0, j)),
                  pl.BlockSpec((1, tn), lambda j: (0, j))],
        out_specs=pl.BlockSpec((bsz, tn), lambda j: (0, j)),
        out_shape=jax.ShapeDtypeStruct((bsz, n), F32),
        compiler_params=_cparams(),
        name="ada_mod",
    )(c, w, b[None])


def _rope_tables(seq):
    half = ROT_DIM // 2
    inv_freq = 1.0 / (ROPE_THETA ** (jnp.arange(0, ROT_DIM, 2, dtype=F32) / ROT_DIM))
    pos = jnp.arange(seq, dtype=F32)
    d = np.arange(LANES) % HEAD_DIM
    ang_l = pos[:, None] * jnp.tile(inv_freq, LANES // half)[None, :]
    cos_f, sin_f = jnp.cos(ang_l), jnp.sin(ang_l)
    cos_l = jnp.where(jnp.asarray(d < ROT_DIM), cos_f, 1.0)
    sin_a = jnp.where(jnp.asarray(d < half), -sin_f, 0.0)
    sin_b = jnp.where(jnp.asarray((d >= half) & (d < ROT_DIM)), sin_f, 0.0)
    ang_t = inv_freq[:, None] * pos[None, :]
    return cos_l, sin_a, sin_b, jnp.cos(ang_t), jnp.sin(ang_t)


def _inproj_kernel(x_ref, sh_ref, sc_ref, g_ref, wm_ref, wt_ref, cl_ref, sa_ref, sb_ref, ct_ref, st_ref,
                   u_ref, k_ref, km_ref, ga_ref, gb_ref, qt_ref, vt_ref, *, sw, aw):
    tm, d = x_ref.shape[1], x_ref.shape[2]
    half = ROT_DIM // 2
    xf = x_ref[0]
    r = lax.rsqrt(jnp.mean(xf * xf, axis=-1, keepdims=True) + RMS_EPS)
    h = (xf * r * g_ref[...]) * (1.0 + sc_ref[0]) + sh_ref[0]
    hb = h.astype(BF16)
    pm = jnp.dot(hb, wm_ref[...], preferred_element_type=F32)
    for q in range(sw // LANES):
        u_ref[0, q] = pm[:, q * LANES:(q + 1) * LANES]
    ga_ref[0] = jax.nn.sigmoid(pm[:, sw + aw:sw + aw + d]).astype(BF16)
    gb_ref[0] = jax.nn.sigmoid(pm[:, sw + aw + d:]).astype(BF16)

    cl, sa, sb = cl_ref[...], sa_ref[...], sb_ref[...]
    parts = []
    for t in range(aw // LANES):
        kt = pm[:, sw + t * LANES:sw + (t + 1) * LANES]
        parts.append(kt * cl + pltpu.roll(kt, LANES - half, 1) * sa + pltpu.roll(kt, half, 1) * sb)
    krot = jnp.concatenate(parts, axis=1)
    k_ref[0] = krot.astype(BF16)
    km_ref[0, 0] = jnp.mean(krot.reshape(tm // MOBA_BLOCK, MOBA_BLOCK, aw), axis=1)

    qv = lax.dot_general(wt_ref[...], hb, (((1,), (1,)), ((), ())), preferred_element_type=F32)
    ct, st = ct_ref[...], st_ref[...]
    rows = []
    for hd in range(aw // HEAD_DIM):
        base = hd * HEAD_DIM
        t1 = qv[base:base + half]
        t2 = qv[base + half:base + ROT_DIM]
        rows += [t1 * ct - t2 * st, t2 * ct + t1 * st, qv[base + ROT_DIM:base + HEAD_DIM]]
    qb = (jnp.concatenate(rows, axis=0) * QK_SCALE).astype(BF16)
    vb = qv[aw:].astype(BF16)
    for cb in range(tm // MOBA_BLOCK):
        qt_ref[0, cb] = qb[:, cb * MOBA_BLOCK:(cb + 1) * MOBA_BLOCK]
        vt_ref[0, cb] = vb[:, cb * MOBA_BLOCK:(cb + 1) * MOBA_BLOCK]


def in_proj(x, shift, scale, g, w_in, sw, aw, tm=512):
    bsz, seq, d = x.shape
    tm = min(tm, seq)
    nb = seq // MOBA_BLOCK
    wm = jnp.concatenate([w_in[:, :sw], w_in[:, sw + aw:sw + 2 * aw], w_in[:, sw + 3 * aw:]], axis=1).astype(BF16)
    wt = jnp.concatenate([w_in[:, sw:sw + aw], w_in[:, sw + 2 * aw:sw + 3 * aw]], axis=1).T.astype(BF16)
    cos_l, sin_a, sin_b, cos_t, sin_t = _rope_tables(seq)
    tok = lambda w: pl.BlockSpec((1, tm, w), lambda b, i: (b, i, 0))
    vec = pl.BlockSpec((1, 1, d), lambda b, i: (b, 0, 0))
    tab = pl.BlockSpec((tm, LANES), lambda b, i: (i, 0))
    tabt = pl.BlockSpec((ROT_DIM // 2, tm), lambda b, i: (0, i))
    blkt = pl.BlockSpec((1, tm // MOBA_BLOCK, aw, MOBA_BLOCK), lambda b, i: (b, i, 0, 0))
    return pl.pallas_call(
        functools.partial(_inproj_kernel, sw=sw, aw=aw),
        grid=(bsz, seq // tm),
        in_specs=[tok(d), vec, vec, _const_spec((1, d)), _const_spec(wm.shape), _const_spec(wt.shape),
                  tab, tab, tab, tabt, tabt],
        out_specs=[pl.BlockSpec((1, sw // LANES, tm, LANES), lambda b, i: (b, 0, i, 0)), tok(aw),
                   pl.BlockSpec((1, 1, tm // MOBA_BLOCK, aw), lambda b, i: (b, i, 0, 0)),
                   tok(d), tok(d), blkt, blkt],
        out_shape=[jax.ShapeDtypeStruct((bsz, sw // LANES, seq, LANES), F32),
                   jax.ShapeDtypeStruct((bsz, seq, aw), BF16),
                   jax.ShapeDtypeStruct((bsz, seq // tm, tm // MOBA_BLOCK, aw), F32),
                   jax.ShapeDtypeStruct((bsz, seq, d), BF16),
                   jax.ShapeDtypeStruct((bsz, seq, d), BF16),
                   jax.ShapeDtypeStruct((bsz, nb, aw, MOBA_BLOCK), BF16),
                   jax.ShapeDtypeStruct((bsz, nb, aw, MOBA_BLOCK), BF16)],
        compiler_params=_cparams(),
        name="in_proj",
    )(x, shift[:, None], scale[:, None], g[None], wm, wt, cos_l, sin_a, sin_b, cos_t, sin_t)


ATTN_GROUP = 4
SCORE_ROWS = 32
ATTN_UNROLL = 2


def _attn_kernel(q_ref, k_ref, v_ref, km_ref, o_ref, qs_ref, bias_ref, s0_ref, s1_ref, p0_ref, p1_ref, acc_ref):
    i = pl.program_id(2)
    nb = km_ref.shape[1]
    blk = MOBA_BLOCK
    gw = ATTN_GROUP * HEAD_DIM
    n_slab = blk // SCORE_ROWS
    q4 = q_ref[0, 0]
    rowid = lax.broadcasted_iota(jnp.int32, (gw, 1), 0)
    kmb = km_ref[0].astype(BF16)
    bid = lax.broadcasted_iota(jnp.int32, (nb, 1), 0)
    for h in range(ATTN_GROUP):
        in_head = (rowid >= h * HEAD_DIM) & (rowid < (h + 1) * HEAD_DIM)
        qh = jnp.where(in_head, q4, jnp.zeros_like(q4))
        qs_ref[h] = qh
        gate = jnp.dot(kmb, qh, preferred_element_type=F32)
        g = jnp.where(bid < i, gate, -jnp.inf)
        bias = jnp.full(g.shape, NEG, F32)
        for _ in range(MOBA_TOPK):
            m = jnp.max(g, axis=0, keepdims=True)
            first = jnp.min(jnp.where(g == m, bid, nb), axis=0, keepdims=True)
            hit = (bid == first) & (m > -jnp.inf)
            bias = jnp.where(hit, 0.0, bias)
            g = jnp.where(hit, -jnp.inf, g)
        bias_ref[h] = bias

    kpos = lax.broadcasted_iota(jnp.int32, (SCORE_ROWS, blk), 0)
    qpos = lax.broadcasted_iota(jnp.int32, (SCORE_ROWS, blk), 1)

    def scores(j, s_ref):
        for h in range(ATTN_GROUP):
            s_ref[h] = jnp.dot(k_ref[0, pl.ds(pl.multiple_of(j * blk, blk), blk), :], qs_ref[h],
                               preferred_element_type=F32)

    def weighted_values(j, p_ref):
        return [jnp.dot(v_ref[0, j, h * HEAD_DIM:(h + 1) * HEAD_DIM, :], p_ref[h], preferred_element_type=F32)
                for h in range(ATTN_GROUP)]

    def block_update(j, s_ref, p_ref, ms, ls, own, nxt=None, pend=None):
        if nxt is not None:
            scores(*nxt)
        if pend is not None:
            pvs = weighted_values(pend[0], pend[1])
        ms_new, ls_new, scales = [], [], []
        for h in range(ATTN_GROUP):
            def slab(c):
                s = s_ref[h, c * SCORE_ROWS:(c + 1) * SCORE_ROWS, :]
                if own:
                    s = jnp.where(kpos + c * SCORE_ROWS <= qpos, s, NEG)
                return s
            cm = slab(0)
            for c in range(1, n_slab):
                cm = jnp.maximum(cm, slab(c))
            cm = jnp.max(cm, axis=0, keepdims=True)
            if own:
                mn, shift = cm, cm
                scales.append(jnp.zeros_like(cm))
            else:
                row = bias_ref[h, pl.ds(j, 1), :]
                mn = jnp.maximum(ms[h], cm + row)
                shift = mn - row
                scales.append(jnp.exp2(ms[h] - mn))
            psum = jnp.zeros((SCORE_ROWS, blk), F32)
            for c in range(n_slab):
                p = jnp.exp2(slab(c) - shift)
                psum = psum + p
                p_ref[h, c * SCORE_ROWS:(c + 1) * SCORE_ROWS, :] = p.astype(BF16)
            lsum = jnp.sum(psum, axis=0, keepdims=True)
            ms_new.append(mn)
            ls_new.append(lsum if own else scales[h] * ls[h] + lsum)
        if pend is not None:
            for h in range(ATTN_GROUP):
                acc_ref[h] = pend[2][h] * acc_ref[h] + pvs[h]
        return ms_new, ls_new, scales

    acc_ref[...] = jnp.zeros(acc_ref.shape, F32)
    scores(i, s1_ref)
    scores(0, s0_ref)
    ms, ls, sc = block_update(i, s1_ref, p1_ref, None, None, True)
    g = ATTN_GROUP

    bufs = ((s0_ref, p0_ref), (s1_ref, p1_ref))

    def body(t, carry):
        ms, ls, sc = carry[:g], carry[g:2 * g], carry[2 * g:]
        for un in range(ATTN_UNROLL):
            j = ATTN_UNROLL * t + un
            prev = jnp.where(j == 0, i, jnp.minimum(j - 1, i))
            s_cur, p_cur = bufs[un % 2]
            s_oth, p_oth = bufs[1 - un % 2]
            ms, ls, sc = block_update(jnp.minimum(j, i), s_cur, p_cur, ms, ls, False,
                                      nxt=(jnp.minimum(j + 1, i), s_oth), pend=(prev, p_oth, sc))
        return tuple(ms) + tuple(ls) + tuple(sc)

    n_trips = (i + ATTN_UNROLL - 1) // ATTN_UNROLL
    carry = lax.fori_loop(0, n_trips, body, tuple(ms) + tuple(ls) + tuple(sc))
    last = jnp.where(n_trips == 0, i, jnp.minimum(ATTN_UNROLL * n_trips - 1, i))
    pvs = weighted_values(last, p1_ref)
    for h in range(g):
        acc = carry[2 * g + h] * acc_ref[h] + pvs[h]
        o_ref[0, 0, h * HEAD_DIM:(h + 1) * HEAD_DIM, :] = (acc / carry[g + h]).astype(BF16)


def moba_attn(qt, k, vt, kmean):
    bsz, nb, aw, blk = qt.shape
    seq = k.shape[1]
    gw = ATTN_GROUP * HEAD_DIM
    return pl.pallas_call(
        _attn_kernel,
        grid=(bsz, aw // gw, nb),
        in_specs=[pl.BlockSpec((1, 1, gw, blk), lambda b, p, i: (b, i, p, 0)),
                  pl.BlockSpec((1, seq, gw), lambda b, p, i: (b, 0, p)),
                  pl.BlockSpec((1, nb, gw, blk), lambda b, p, i: (b, 0, p, 0)),
                  pl.BlockSpec((1, nb, gw), lambda b, p, i: (b, 0, p))],
        out_specs=pl.BlockSpec((1, 1, gw, blk), lambda b, p, i: (b, i, p, 0)),
        out_shape=jax.ShapeDtypeStruct((bsz, nb, aw, blk), BF16),
        scratch_shapes=[pltpu.VMEM((ATTN_GROUP, gw, blk), BF16),
                        pltpu.VMEM((ATTN_GROUP, nb, blk), F32),
                        pltpu.VMEM((ATTN_GROUP, blk, blk), F32),
                        pltpu.VMEM((ATTN_GROUP, blk, blk), F32),
                        pltpu.VMEM((ATTN_GROUP, blk, blk), BF16),
                        pltpu.VMEM((ATTN_GROUP, blk, blk), BF16),
                        pltpu.VMEM((ATTN_GROUP, HEAD_DIM, blk), F32)],
        compiler_params=_cparams(),
        name="moba_attn",
    )(qt, k, vt, kmean)


def _ssm_matrices(a_re, a_im, log_dt, b_re, b_im, c_re, c_im):
    grp, p_st = a_re.shape
    ch = b_re.shape[-1]
    gpt = LANES // ch
    nq = grp // gpt
    dt = jnp.exp(log_dt.astype(F32))[:, None]
    lr, li = a_re.astype(F32), a_im.astype(F32)

    def power(n):
        mag = jnp.exp(lr * dt * n)
        return mag * jnp.cos(li * dt * n), mag * jnp.sin(li * dt * n)

    abar_r, abar_i = power(1.0)
    at_r, at_i = power(float(SSM_CHUNK))
    den = lr * lr + li * li
    coef_r = ((abar_r - 1.0) * lr + abar_i * li) / den
    coef_i = (abar_i * lr - (abar_r - 1.0) * li) / den
    br, bi = b_re.astype(F32), b_im.astype(F32)
    bbar_r = coef_r[..., None] * br - coef_i[..., None] * bi
    bbar_i = coef_r[..., None] * bi + coef_i[..., None] * br
    eye = jnp.eye(gpt, dtype=F32)

    def diag_in(m):
        m = m.reshape(nq, gpt, p_st, ch)
        return jnp.einsum('qgpc,gh->qgchp', m, eye).reshape(nq, gpt * ch, gpt * p_st)

    def diag_out(m):
        m = m.reshape(nq, gpt, ch, p_st)
        return jnp.einsum('qgcp,gh->qgphc', m, eye).reshape(nq, gpt * p_st, gpt * ch)

    bd = jnp.concatenate([diag_in(bbar_r), diag_in(bbar_i)], axis=2)
    cd_r, cd_i = diag_out(c_re.astype(F32)), diag_out(-c_im.astype(F32))
    tile = lambda m: m.reshape(nq, 1, gpt * p_st)
    return tile(abar_r), tile(abar_i), at_r, at_i, bd, cd_r, cd_i


def _ssm_chunk_scan(u_ref, bd_ref, ar_ref, ai_ref, xr_ref, xi_ref, q, emit):
    n_rows, width = xr_ref.shape
    ar, ai = ar_ref[q], ai_ref[q]

    def body(j, carry):
        uj = u_ref[0, q, pl.ds(j, n_rows, stride=SSM_CHUNK), :].astype(BF16)
        bu = jnp.dot(uj, bd_ref[q], preferred_element_type=F32)
        xr, xi = xr_ref[...], xi_ref[...]
        nr = ar * xr - ai * xi + bu[:, :width]
        ni = ar * xi + ai * xr + bu[:, width:]
        xr_ref[...] = nr
        xi_ref[...] = ni
        if emit is not None:
            emit(j, nr, ni)
        return carry

    lax.fori_loop(0, SSM_CHUNK, body, 0)


def _ssm_state_kernel(u_ref, bd_ref, ar_ref, ai_ref, sr_ref, si_ref, xr_ref, xi_ref):
    width = xr_ref.shape[1]
    for q in range(bd_ref.shape[0]):
        xr_ref[...] = jnp.zeros(xr_ref.shape, F32)
        xi_ref[...] = jnp.zeros(xi_ref.shape, F32)
        _ssm_chunk_scan(u_ref, bd_ref, ar_ref, ai_ref, xr_ref, xi_ref, q, None)
        sr_ref[:, q * width:(q + 1) * width] = xr_ref[...]
        si_ref[:, q * width:(q + 1) * width] = xi_ref[...]


def _ssm_scan_kernel(sr_ref, si_ref, ar_ref, ai_ref, xr_ref, xi_ref):
    nc = sr_ref.shape[0]
    ar, ai = ar_ref[...], ai_ref[...]

    def body(c, carry):
        xr, xi = carry
        row = pl.ds(c, 1)
        xr_ref[row, :] = xr
        xi_ref[row, :] = xi
        return (ar * xr - ai * xi + sr_ref[row, :], ar * xi + ai * xr + si_ref[row, :])

    zero = jnp.zeros(ar.shape, F32)
    lax.fori_loop(0, nc, body, (zero, zero), unroll=8)


def _ssm_out_kernel(u_ref, bd_ref, ar_ref, ai_ref, cr_ref, ci_ref, x0r_ref, x0i_ref, y_ref, xr_ref, xi_ref):
    n_rows, width = xr_ref.shape
    for q in range(bd_ref.shape[0]):
        xr_ref[...] = x0r_ref[:, q * width:(q + 1) * width]
        xi_ref[...] = x0i_ref[:, q * width:(q + 1) * width]

        def emit(j, xr, xi, q=q):
            y = (jnp.dot(xr.astype(BF16), cr_ref[q], preferred_element_type=F32)
                 + jnp.dot(xi.astype(BF16), ci_ref[q], preferred_element_type=F32))
            y_ref[0, q, pl.ds(j, n_rows, stride=SSM_CHUNK), :] = y

        _ssm_chunk_scan(u_ref, bd_ref, ar_ref, ai_ref, xr_ref, xi_ref, q, emit)


def ssm_mixer(u, a_re, a_im, log_dt, b_re, b_im, c_re, c_im, ts=4096):
    bsz, _, seq, _ = u.shape
    ts = min(ts, seq)
    nc = seq // SSM_CHUNK
    rows = ts // SSM_CHUNK
    abar_r, abar_i, at_r, at_i, bd, cd_r, cd_i = _ssm_matrices(a_re, a_im, log_dt, b_re, b_im, c_re, c_im)
    nq, _, width = abar_r.shape
    bd, cd_r, cd_i = bd.astype(BF16), cd_r.astype(BF16), cd_i.astype(BF16)
    ncols = bsz * nq * width
    u_spec = pl.BlockSpec((1, nq, ts, LANES), lambda b, i: (b, 0, i, 0))
    s_spec = pl.BlockSpec((rows, nq * width), lambda b, i: (i, b))
    consts = [_const_spec(bd.shape), _const_spec(abar_r.shape), _const_spec(abar_i.shape)]
    x_scratch = [pltpu.VMEM((rows, width), F32)] * 2
    s_re, s_im = pl.pallas_call(
        _ssm_state_kernel,
        grid=(bsz, seq // ts),
        in_specs=[u_spec] + consts,
        out_specs=[s_spec, s_spec],
        out_shape=[jax.ShapeDtypeStruct((nc, ncols), F32)] * 2,
        scratch_shapes=x_scratch,
        compiler_params=_cparams(),
        name="ssm_state",
    )(u, bd, abar_r, abar_i)

    lw = SUBLANES * LANES
    a_r = jnp.tile(at_r.reshape(-1), bsz).reshape(1, ncols)
    a_i = jnp.tile(at_i.reshape(-1), bsz).reshape(1, ncols)
    sc_spec = pl.BlockSpec((nc, lw), lambda j: (0, j))
    a_spec = pl.BlockSpec((1, lw), lambda j: (0, j))
    x_re, x_im = pl.pallas_call(
        _ssm_scan_kernel,
        grid=(ncols // lw,),
        in_specs=[sc_spec, sc_spec, a_spec, a_spec],
        out_specs=[sc_spec, sc_spec],
        out_shape=[jax.ShapeDtypeStruct((nc, ncols), F32)] * 2,
        compiler_params=_cparams(),
        name="ssm_scan",
    )(s_re, s_im, a_r, a_i)

    return pl.pallas_call(
        _ssm_out_kernel,
        grid=(bsz, seq // ts),
        in_specs=[u_spec] + consts + [_const_spec(cd_r.shape), _const_spec(cd_i.shape), s_spec, s_spec],
        out_specs=u_spec,
        out_shape=jax.ShapeDtypeStruct(u.shape, F32),
        scratch_shapes=x_scratch,
        compiler_params=_cparams(),
        name="ssm_out",
    )(u, bd, abar_r, abar_i, cd_r, cd_i, x_re, x_im)


def _merge_kernel(x_ref, y_ref, u_ref, dsk_ref, ot_ref, ga_ref, gb_ref, wglu_ref, wo_ref, wout_ref,
                  gt_ref, sh_ref, sc_ref, g_ref, wr_ref, br_ref,
                  x1_ref, h2_ref, ti_ref, tw_ref):
    d = x_ref.shape[2]
    n_exp = wr_ref.shape[0]
    nq = y_ref.shape[1]
    y = (jnp.concatenate([y_ref[0, q] for q in range(nq)], axis=1)
         + dsk_ref[...] * jnp.concatenate([u_ref[0, q] for q in range(nq)], axis=1))
    a = jax.nn.gelu(y, approximate=True).astype(BF16)
    glu = jnp.dot(a, wglu_ref[...], preferred_element_type=F32)
    y_a = glu[:, :d] * jax.nn.sigmoid(glu[:, d:])
    y_b = jnp.concatenate(
        [lax.dot_general(ot_ref[0, cb], wo_ref[...], (((0,), (0,)), ((), ())), preferred_element_type=F32)
         for cb in range(ot_ref.shape[1])], axis=0)
    merged = ga_ref[0].astype(F32) * y_a + gb_ref[0].astype(F32) * y_b
    z = jnp.dot(merged.astype(BF16), wout_ref[...], preferred_element_type=F32)
    x1 = x_ref[0] + gt_ref[0] * z
    x1_ref[0] = x1
    r = lax.rsqrt(jnp.mean(x1 * x1, axis=-1, keepdims=True) + RMS_EPS)
    h2 = (x1 * r * g_ref[...]) * (1.0 + sc_ref[0]) + sh_ref[0]
    h2_ref[0] = h2
    logits = lax.dot_general(wr_ref[...], h2.astype(BF16), (((1,), (1,)), ((), ())),
                             preferred_element_type=F32) + br_ref[...]
    eid = lax.broadcasted_iota(jnp.int32, (n_exp, 1), 0)
    vals, idxs = [], []
    g = logits
    for _ in range(TOP_K):
        m = jnp.max(g, axis=0, keepdims=True)
        first = jnp.min(jnp.where(g == m, eid, n_exp), axis=0, keepdims=True)
        vals.append(m)
        idxs.append(first)
        g = jnp.where(eid == first, -jnp.inf, g)
    v = jnp.concatenate(vals, axis=0)
    e = jnp.exp(v - v[0:1])
    tw_ref[0] = e / jnp.sum(e, axis=0, keepdims=True)
    ti_ref[0] = jnp.concatenate(idxs, axis=0)


def merge(x, y_ssm, u, d_skip, ot, ga, gb, w_glu, w_attn_o, w_out, gate, shift, scale, g, w_router, b_router,
          tm=256):
    bsz, seq, d = x.shape
    nq = u.shape[1]
    sw = nq * LANES
    aw = ot.shape[2]
    n_exp = w_router.shape[1]
    tm = min(tm, seq)
    tok = lambda w: pl.BlockSpec((1, tm, w), lambda b, i: (b, i, 0))
    tiles = pl.BlockSpec((1, nq, tm, LANES), lambda b, i: (b, 0, i, 0))
    vec = pl.BlockSpec((1, 1, d), lambda b, i: (b, 0, 0))
    sel = pl.BlockSpec((1, TOP_K, tm), lambda b, i: (b, 0, i))
    return pl.pallas_call(
        _merge_kernel,
        grid=(bsz, seq // tm),
        in_specs=[tok(d), tiles, tiles, _const_spec((1, sw)),
                  pl.BlockSpec((1, tm // MOBA_BLOCK, aw, MOBA_BLOCK), lambda b, i: (b, i, 0, 0)),
                  tok(d), tok(d),
                  _const_spec(w_glu.shape), _const_spec(w_attn_o.shape), _const_spec(w_out.shape),
                  vec, vec, vec, _const_spec((1, d)), _const_spec((n_exp, d)), _const_spec((n_exp, 1))],
        out_specs=[tok(d), tok(d), sel, sel],
        out_shape=[jax.ShapeDtypeStruct((bsz, seq, d), F32),
                   jax.ShapeDtypeStruct((bsz, seq, d), F32),
                   jax.ShapeDtypeStruct((bsz, TOP_K, seq), jnp.int32),
                   jax.ShapeDtypeStruct((bsz, TOP_K, seq), F32)],
        compiler_params=_cparams(),
        name="merge",
    )(x, y_ssm, u, d_skip[None], ot, ga, gb, w_glu.astype(BF16), w_attn_o.astype(BF16), w_out.astype(BF16),
      gate[:, None], shift[:, None], scale[:, None], g[None], w_router.T.astype(BF16), b_router[:, None])


MOE_CHUNK = 256


def _moe_kernel(blk_e_ref, tok_hbm, dst_hbm, w_ref, h_hbm, wgu_ref, bgu_ref, wd_ref, bd_ref,
                out_hbm, tok_s, dst_s, xbuf, ybuf, xb_ref, act_ref, w1_ref, w2_ref, sem_idx, sem_g, sem_s,
                *, n_blocks):
    i = pl.program_id(0)
    rows = xbuf.shape[1]
    f = wd_ref.shape[1]

    def tok_copy(b, s):
        return pltpu.make_async_copy(tok_hbm.at[b], tok_s.at[s], sem_idx.at[s])

    def dst_copy(b, s):
        return pltpu.make_async_copy(dst_hbm.at[b], dst_s.at[s], sem_idx.at[2 + s])

    def gather_row(r, s):
        return pltpu.make_async_copy(h_hbm.at[pl.ds(tok_s[s, 0, r], 1)], xbuf.at[s, pl.ds(r, 1)], sem_g.at[s])

    def scatter_row(r, s):
        return pltpu.make_async_copy(ybuf.at[s, pl.ds(r, 1)], out_hbm.at[pl.ds(dst_s[s, 0, r], 1)], sem_s.at[s])

    def gather_all(s):
        return pltpu.make_async_copy(h_hbm.at[pl.ds(0, rows)], xbuf.at[s], sem_g.at[s])

    def scatter_all(s):
        return pltpu.make_async_copy(ybuf.at[s], out_hbm.at[pl.ds(0, rows)], sem_s.at[s])

    @pl.when(i == 0)
    def _():
        tok_copy(0, 0).start()
        tok_copy(0, 0).wait()

        def first_rows(r, carry):
            gather_row(r, 0).start()
            return carry

        lax.fori_loop(0, rows, first_rows, 0)
        tok_copy(1, 1).start()
        dst_copy(n_blocks, 1).start()
        ybuf[1] = jnp.zeros(ybuf.shape[1:], F32)

    def step(slot):
        oslot = 1 - slot
        tok_copy(0, oslot).wait()
        dst_copy(0, oslot).wait()
        tok_copy(i + 2, slot).start()
        dst_copy(i, slot).start()
        gather_all(slot).wait()

        @pl.when((i == 0) | (blk_e_ref[i] != blk_e_ref[jnp.maximum(i - 1, 0)]))
        def _():
            w1_ref[...] = wgu_ref[0].astype(BF16)
            w2_ref[...] = wd_ref[0].astype(BF16)

        xb_ref[...] = xbuf[slot].astype(BF16)
        for r in range(rows):
            gather_row(r, oslot).start(priority=r % 2)

        for c in range(f // MOE_CHUNK):
            lo, hi = c * MOE_CHUNK, (c + 1) * MOE_CHUNK
            xb = xb_ref[...]
            gate = jnp.dot(xb, w1_ref[:, lo:hi], preferred_element_type=F32) + bgu_ref[0, :, lo:hi]
            up = (jnp.dot(xb, w1_ref[:, f + lo:f + hi], preferred_element_type=F32)
                  + bgu_ref[0, :, f + lo:f + hi])
            gate = jnp.minimum(gate, SWIGLU_LIMIT)
            up = jnp.clip(up, -SWIGLU_LIMIT, SWIGLU_LIMIT)
            act_ref[:, lo:hi] = ((up + 1.0) * gate * jax.nn.sigmoid(SWIGLU_ALPHA * gate)).astype(BF16)
        if slot == 0:
            pl.when(i > 0)(lambda: scatter_all(slot).wait())
        else:
            scatter_all(slot).wait()
        for r in range(rows):
            scatter_row(r, oslot).start(priority=r % 2)
        ybuf[slot] = (jnp.dot(act_ref[...], w2_ref[...], preferred_element_type=F32) + bd_ref[0]) * w_ref[0]

    for s in range(2):
        pl.when(i % 2 == s)(functools.partial(step, s))

    @pl.when(i == n_blocks - 1)
    def _():
        slot = (n_blocks - 1) % 2
        tok_copy(0, slot).wait()
        dst_copy(0, slot).wait()
        gather_all(1 - slot).wait()

        def last_rows(r, carry):
            scatter_row(r, slot).start()
            return carry

        lax.fori_loop(0, rows, last_rows, 0)
        scatter_all(1 - slot).wait()
        scatter_all(slot).wait()


def moe_experts(h2, top_idx, top_w, w_gu, b_gu, w_down, b_down):
    n, d = h2.shape
    n_exp, _, f2 = w_gu.shape
    f = f2 // 2
    rb = EXPERT_ROWS
    nk = n * TOP_K
    i32 = jnp.int32
    flat_e = top_idx.reshape(nk)
    flat_w = top_w.reshape(nk)
    order = jnp.argsort(flat_e).astype(i32)
    counts = jnp.sum((flat_e[:, None] == jnp.arange(n_exp, dtype=i32)[None, :]).astype(i32), axis=0)
    group_start = jnp.cumsum(counts) - counts
    padded = (counts + rb - 1) // rb * rb
    padded_end = jnp.cumsum(padded)
    padded_start = padded_end - padded
    n_blocks = -(-nk // rb) + n_exp
    rows_total = n_blocks * rb
    blk_start = jnp.arange(n_blocks, dtype=i32) * rb
    blk_e = jnp.minimum(jnp.sum((padded_end[None, :] <= blk_start[:, None]).astype(i32), axis=1), n_exp - 1)
    row = jnp.arange(rows_total, dtype=i32)
    row_e = jnp.repeat(blk_e, rb)
    local = row - padded_start[row_e]
    valid = local < counts[row_e]
    pair = order[jnp.clip(group_start[row_e] + local, 0, nk - 1)]
    row_tok = jnp.where(valid, pair % n, 0)
    row_w = jnp.where(valid, flat_w[pair], 0.0)
    pad_rank = jnp.cumsum((~valid).astype(i32)) - 1
    row_dst = jnp.where(valid, pair, nk + pad_rank)
    tok_rows = jnp.concatenate([row_tok, jnp.zeros((2 * rb,), i32)]).reshape(n_blocks + 2, 1, rb)
    dst_rows = jnp.concatenate([row_dst, rows_total + jnp.arange(rb, dtype=i32)]).reshape(n_blocks + 1, 1, rb)

    grid_spec = pltpu.PrefetchScalarGridSpec(
        num_scalar_prefetch=1,
        grid=(n_blocks,),
        in_specs=[pl.BlockSpec(memory_space=pl.ANY),
                  pl.BlockSpec(memory_space=pl.ANY),
                  pl.BlockSpec((1, rb, 1), lambda i, be: (i, 0, 0)),
                  pl.BlockSpec(memory_space=pl.ANY),
                  pl.BlockSpec((1, d, f2), lambda i, be: (be[i], 0, 0)),
                  pl.BlockSpec((1, 1, f2), lambda i, be: (be[i], 0, 0)),
                  pl.BlockSpec((1, f, d), lambda i, be: (be[i], 0, 0)),
                  pl.BlockSpec((1, 1, d), lambda i, be: (be[i], 0, 0))],
        out_specs=pl.BlockSpec(memory_space=pl.ANY),
        scratch_shapes=[pltpu.SMEM((2, 1, rb), i32), pltpu.SMEM((2, 1, rb), i32),
                        pltpu.VMEM((2, rb, d), F32), pltpu.VMEM((2, rb, d), F32),
                        pltpu.VMEM((rb, d), BF16), pltpu.VMEM((rb, f), BF16),
                        pltpu.VMEM((d, f2), BF16), pltpu.VMEM((f, d), BF16),
                        pltpu.SemaphoreType.DMA((4,)), pltpu.SemaphoreType.DMA((2,)),
                        pltpu.SemaphoreType.DMA((2,))])
    return pl.pallas_call(
        functools.partial(_moe_kernel, n_blocks=n_blocks),
        grid_spec=grid_spec,
        out_shape=jax.ShapeDtypeStruct((rows_total + rb, d), F32),
        compiler_params=_cparams(),
        name="moe_experts",
    )(blk_e, tok_rows, dst_rows, row_w.reshape(n_blocks, rb, 1),
      h2, w_gu, b_gu[:, None], w_down, b_down[:, None])


def _combine_kernel(x_ref, *refs):
    y_refs, (gt_ref, g_ref, o_ref) = refs[:TOP_K], refs[TOP_K:]
    moe = y_refs[0][...]
    for y_ref in y_refs[1:]:
        moe = moe + y_ref[...]
    x2 = x_ref[0] + gt_ref[0] * moe
    r = lax.rsqrt(jnp.mean(x2 * x2, axis=-1, keepdims=True) + RMS_EPS)
    o_ref[0] = x2 * r * g_ref[...]


def combine(x1, ytk, gate, g, tm=256):
    bsz, seq, d = x1.shape
    tm = min(tm, seq)
    nt = seq // tm
    choice = lambda k: pl.BlockSpec((tm, d), lambda b, i: ((k * bsz + b) * nt + i, 0))
    return pl.pallas_call(
        _combine_kernel,
        grid=(bsz, nt),
        in_specs=[pl.BlockSpec((1, tm, d), lambda b, i: (b, i, 0))] + [choice(k) for k in range(TOP_K)]
                 + [pl.BlockSpec((1, 1, d), lambda b, i: (b, 0, 0)), _const_spec((1, d))],
        out_specs=pl.BlockSpec((1, tm, d), lambda b, i: (b, i, 0)),
        out_shape=jax.ShapeDtypeStruct((bsz, seq, d), F32),
        compiler_params=_cparams(),
        name="combine",
    )(x1, *([ytk] * TOP_K), gate[:, None], g[None])


def kernel(x, c, w_ada, b_ada, g_mix, w_in, ssm_a_re, ssm_a_im, ssm_log_dt, ssm_b_re, ssm_b_im, ssm_c_re,
           ssm_c_im, ssm_d, w_glu, w_attn_o, w_out, g_ffn, w_router, b_router, w_gu, b_gu, w_down, b_down,
           g_final):
    bsz, seq, d = x.shape
    depth = w_ada.shape[0]
    assert depth == 1
    sw = ssm_d.shape[1]
    aw = w_attn_o.shape[1]
    assert seq % MOBA_BLOCK == 0 and seq % SSM_CHUNK == 0
    for l in range(depth):
        mod = ada_mod(c, w_ada[l], b_ada[l])
        sh1, sc1, gt1, sh2, sc2, gt2 = jnp.split(mod, 6, axis=-1)
        u, k, kmean, ga, gb, qt, vt = in_proj(x, sh1, sc1, g_mix[l], w_in[l], sw, aw)
        ot = moba_attn(qt, k, vt, kmean.reshape(bsz, seq // MOBA_BLOCK, aw))
        y_ssm = ssm_mixer(u, ssm_a_re[l], ssm_a_im[l], ssm_log_dt[l], ssm_b_re[l], ssm_b_im[l],
                          ssm_c_re[l], ssm_c_im[l])
        x1, h2, top_idx, top_w = merge(x, y_ssm, u, ssm_d[l], ot, ga, gb, w_glu[l], w_attn_o[l], w_out[l],
                                       gt1, sh2, sc2, g_ffn[l], w_router[l], b_router[l])
        n = bsz * seq
        ytk = moe_experts(h2.reshape(n, d), top_idx.transpose(1, 0, 2).reshape(TOP_K, n),
                          top_w.transpose(1, 0, 2).reshape(TOP_K, n), w_gu[l], b_gu[l], w_down[l], b_down[l])
        x = combine(x1, ytk, gt2, g_final)
    return x
```
